```python
import math
import jax, jax.numpy as jnp
from jax import lax
import numpy as np

D_MODEL = 1024
BATCH = 2
SEQ = 8192
DEPTH = 2
DEC_BATCH = 128
DEC_SEQ = 1
PAST_LEN = 16384
PAGE_SIZE = 128

HEAD_DIM = 64
A_HEADS = 4
A_KV_HEADS = 2
IDX_HEADS = 4
IDX_DIM = 32
TOPK_MAX = 256
REL_BUCKETS = 32
REL_MAX_DIST = 128
B_HEADS = 4
Q_RANK = 256
KV_RANK = 128
NOPE_DIM = 64
ROPE_DIM = 32
V_DIM = 64
ROPE_THETA = 10000.0
A_WIDTH = A_HEADS * HEAD_DIM
B_WIDTH = B_HEADS * V_DIM
C_WIDTH = D_MODEL - A_WIDTH - B_WIDTH
C_GROUP = 16
C_GROUPS = C_WIDTH // C_GROUP
C_STATE = 64
D_FF = 2816
Q_BLOCK = 128
LN_EPS = 1e-5
ALPHA = (2.0 * DEPTH) ** 0.25
BETA = (8.0 * DEPTH) ** -0.25
MLA_SCALE = (NOPE_DIM + ROPE_DIM) ** -0.5
IN_SPLITS = (A_HEADS * HEAD_DIM, A_KV_HEADS * HEAD_DIM, A_KV_HEADS * HEAD_DIM, IDX_HEADS * IDX_DIM, IDX_HEADS, IDX_DIM, Q_RANK, KV_RANK, ROPE_DIM, C_WIDTH)
D_IN = sum(IN_SPLITS)

kernel_name = 'hymba_dsa_mla_s5_decode_step'


def layer_norm(x, g, b):
    xf = x.astype(jnp.float32)
    mu = jnp.mean(xf, -1, keepdims=True)
    var = jnp.mean(jnp.square(xf - mu), -1, keepdims=True)
    return ((xf - mu) * lax.rsqrt(var + LN_EPS) * g + b).astype(x.dtype)


def rms_norm(x, g):
    xf = x.astype(jnp.float32)
    return (xf * lax.rsqrt(jnp.mean(xf * xf, -1, keepdims=True) + 1e-6) * g).astype(x.dtype)


def rope(x, pos):
    half = ROPE_DIM // 2
    freqs = ROPE_THETA ** (-jnp.arange(half, dtype=jnp.float32) / half)
    ang = pos.astype(jnp.float32)[:, None] * freqs
    cos = jnp.cos(ang)[:, None, :]
    sin = jnp.sin(ang)[:, None, :]
    x1 = x[..., :half].astype(jnp.float32)
    x2 = x[..., half:].astype(jnp.float32)
    return jnp.concatenate([x1 * cos - x2 * sin, x1 * sin + x2 * cos], -1).astype(x.dtype)


def t5_bucket(dist):
    n = jnp.maximum(dist, 0)
    max_exact = REL_BUCKETS // 2
    n_f = jnp.maximum(n, 1).astype(jnp.float32)
    log_bucket = max_exact + (jnp.log(n_f / max_exact) / math.log(REL_MAX_DIST / max_exact) * (REL_BUCKETS - max_exact)).astype(jnp.int32)
    return jnp.where(n < max_exact, n, jnp.minimum(log_bucket, REL_BUCKETS - 1))


def swiglu(h, w_up, w_down):
    gu = jnp.einsum('btd,df->btf', h, w_up)
    g, u = jnp.split(gu, 2, axis=-1)
    return jnp.einsum('btf,fd->btd', jax.nn.silu(g) * u, w_down)


def gather_rows(rows, sel):
    return jax.vmap(lambda r, i: r[i])(rows, sel)


def gather_paged_all(pool, l, page_table):
    g = pool[l, page_table]
    return g.reshape(g.shape[0], g.shape[1] * g.shape[2], *g.shape[3:])


def gather_paged_rows(pool, l, page_table, pos):
    db = page_table.shape[0]
    phys = jnp.take_along_axis(page_table, (pos // PAGE_SIZE).reshape(db, -1), axis=1).reshape(pos.shape)
    return pool[l, phys, pos % PAGE_SIZE]


def project_groups(h, pos, w_in, g_qn, g_kvn, w_uq, w_uk):
    b_, t_ = h.shape[0], h.shape[1]
    p = jnp.einsum('btd,de->bte', h, w_in)
    split_at = [int(s) for s in np.cumsum(IN_SPLITS)[:-1]]
    a_q, a_k, a_v, i_q, i_w, i_k, b_cq, b_ckv, b_kpe, c_u = jnp.split(p, split_at, axis=-1)
    a_q = a_q.reshape(b_, t_, A_HEADS, HEAD_DIM)
    a_k = a_k.reshape(b_, t_, A_KV_HEADS, HEAD_DIM)
    a_v = a_v.reshape(b_, t_, A_KV_HEADS, HEAD_DIM)
    i_q = i_q.reshape(b_, t_, IDX_HEADS, IDX_DIM) * (IDX_DIM ** -0.5)
    i_w = i_w * (IDX_HEADS ** -0.5)
    cq = rms_norm(b_cq, g_qn)
    q = jnp.einsum('btr,rhe->bthe', cq, w_uq)
    q_pe = rope(q[..., NOPE_DIM:], pos)
    q_lat = jnp.einsum('bthn,rhn->bthr', q[..., :NOPE_DIM], w_uk)
    ckv = rms_norm(b_ckv, g_kvn)
    kpe = rope(b_kpe[:, :, None, :], pos)[:, :, 0]
    return a_q, a_k, a_v, i_q, i_w, i_k, q_lat, q_pe, ckv, kpe, c_u


def indexer_topk(q_idx, w_idx, k_idx, q_pos, k_pos, topk):
    s = jax.nn.relu(jnp.einsum('bthd,bsd->bths', q_idx, k_idx).astype(jnp.float32))
    score = jnp.einsum('bths,bth->bts', s, w_idx.astype(jnp.float32))
    score = jnp.where(k_pos[None, None, :] <= q_pos[None, :, None], score, -jnp.inf)
    _, sel = lax.top_k(score, topk)
    return sel


def sparse_attend(q, k_sel, v_sel, sel, q_pos, rel_bias):
    b_, t_, n_sel = sel.shape
    grp = A_HEADS // A_KV_HEADS
    qg = q.reshape(b_, t_, A_KV_HEADS, grp, HEAD_DIM)
    logits = jnp.einsum('btcgd,btncd->btcgn', qg, k_sel).astype(jnp.float32) * (HEAD_DIM ** -0.5)
    dist = q_pos[None, :, None] - sel
    bias = rel_bias[t5_bucket(dist)].astype(jnp.float32)
    bias = bias.reshape(b_, t_, n_sel, A_KV_HEADS, grp).transpose(0, 1, 3, 4, 2)
    logits = jnp.where((dist >= 0)[:, :, None, None, :], logits + bias, -jnp.inf)
    p = jax.nn.softmax(logits, axis=-1).astype(v_sel.dtype)
    o = jnp.einsum('btcgn,btncd->btcgd', p, v_sel)
    return o.reshape(b_, t_, A_WIDTH)


def mla_attend(q_lat, q_pe, ckv, kpe, q_pos, k_pos):
    logits = (jnp.einsum('bthr,bsr->bhts', q_lat, ckv) + jnp.einsum('bthe,bse->bhts', q_pe, kpe)).astype(jnp.float32) * MLA_SCALE
    logits = jnp.where(k_pos[None, None, None, :] <= q_pos[None, None, :, None], logits, -jnp.inf)
    p = jax.nn.softmax(logits, axis=-1).astype(ckv.dtype)
    return jnp.einsum('bhts,bsr->bthr', p, ckv)


def s5_mix(u, s0_re, s0_im, lam_re, lam_im, log_dt, b_re, b_im, c_re, c_im, d_skip, w_glu, b_glu):
    f32 = jnp.float32
    b_, t_ = u.shape[0], u.shape[1]
    lam = lax.complex(lam_re.astype(f32), lam_im.astype(f32))
    dt = jnp.exp(log_dt.astype(f32))[:, None]
    a_bar = jnp.exp(lam * dt)
    b_bar = ((a_bar - 1.0) / lam)[..., None] * lax.complex(b_re.astype(f32), b_im.astype(f32))
    c_mat = lax.complex(c_re.astype(f32), c_im.astype(f32))
    ug = u.astype(f32).reshape(b_, t_, C_GROUPS, C_GROUP)
    bu = jnp.einsum('gpc,btgc->btgp', b_bar, ug.astype(jnp.complex64))
    a_seq = jnp.broadcast_to(a_bar, bu.shape)

    def combine(e1, e2):
        a1, v1 = e1
        a2, v2 = e2
        return a2 * a1, a2 * v1 + v2

    a_cum, s = lax.associative_scan(combine, (a_seq, bu), axis=1)
    s0 = lax.complex(s0_re.astype(f32), s0_im.astype(f32))
    s = s + a_cum * s0[:, None]
    y = jnp.einsum('gcp,btgp->btgc', c_mat, s).real + d_skip.astype(f32) * ug
    z = jax.nn.gelu(y.reshape(b_, t_, C_WIDTH))
    out = z * jax.nn.sigmoid(jnp.einsum('bte,ef->btf', z, w_glu) + b_glu)
    s_last = s[:, -1]
    return out.astype(u.dtype), jnp.real(s_last), jnp.imag(s_last)


def run_trunk(x, c, mix_fn, ln_in_g, ln_in_b, w_ada, b_ada, ln_g, ln_b, w_ffn_in, w_ffn_out, w_out):
    x = layer_norm(x, ln_in_g, ln_in_b)
    states = []
    for l in range(DEPTH):
        mod = jnp.einsum('bd,de->be', jax.nn.silu(c), w_ada[l]) + b_ada[l]
        mod = mod.reshape(c.shape[0], 3, 3, 1, D_MODEL)

        def modulate(v, j):
            return v * (1.0 + mod[:, j, 1]) + mod[:, j, 0]

        y = swiglu(modulate(x, 0), w_ffn_in[l, 0], w_ffn_out[l, 0])
        x = layer_norm(ALPHA * x + 0.5 * (1.0 + mod[:, 0, 2]) * y, ln_g[l, 0], ln_b[l, 0])
        mixed, st = mix_fn(modulate(x, 1), l)
        y = jnp.einsum('bte,ed->btd', mixed, w_out[l])
        x = layer_norm(ALPHA * x + (1.0 + mod[:, 1, 2]) * y, ln_g[l, 1], ln_b[l, 1])
        y = swiglu(modulate(x, 2), w_ffn_in[l, 1], w_ffn_out[l, 1])
        x = layer_norm(ALPHA * x + 0.5 * (1.0 + mod[:, 2, 2]) * y, ln_g[l, 2], ln_b[l, 2])
        states.append(st)
    stacked = [jnp.stack(s, axis=0) for s in zip(*states)]
    return x, stacked


def setup_inputs(seed: int = 0) -> dict:
    key = jax.random.key(seed)
    ks = iter(jax.random.split(key, 48))
    f32 = jnp.float32

    def nrm(shape, scale=1.0):
        return jax.random.normal(next(ks), shape, f32) * scale

    n_pages = PAST_LEN // PAGE_SIZE
    n_pool = (DEC_BATCH * n_pages * 5) // 4
    page_table = jax.random.permutation(next(ks), n_pool)[:DEC_BATCH * n_pages].reshape(DEC_BATCH, n_pages).astype(jnp.int32)
    lam_im = jnp.broadcast_to(jnp.pi * jnp.arange(C_STATE, dtype=f32), (DEPTH, C_GROUPS, C_STATE))
    return {
        'x_prompt': nrm((BATCH, SEQ, D_MODEL)),
        'x_sample': nrm((DEC_BATCH, DEC_SEQ, D_MODEL)),
        'c_prompt': nrm((BATCH, D_MODEL)),
        'c_sample': nrm((DEC_BATCH, D_MODEL)),
        'page_table': page_table,
        'cache_attn_k': nrm((DEPTH, n_pool, PAGE_SIZE, A_KV_HEADS, HEAD_DIM)),
        'cache_attn_v': nrm((DEPTH, n_pool, PAGE_SIZE, A_KV_HEADS, HEAD_DIM)),
        'cache_idx_k': nrm((DEPTH, n_pool, PAGE_SIZE, IDX_DIM)),
        'cache_mla_latent': nrm((DEPTH, n_pool, PAGE_SIZE, KV_RANK)),
        'cache_mla_rope': nrm((DEPTH, n_pool, PAGE_SIZE, ROPE_DIM)),
        'state_ssm_re': nrm((DEPTH, DEC_BATCH, C_GROUPS, C_STATE), 0.5),
        'state_ssm_im': nrm((DEPTH, DEC_BATCH, C_GROUPS, C_STATE), 0.5),
        'ln_in_g': 1.0 + nrm((D_MODEL,), 0.01),
        'ln_in_b': nrm((D_MODEL,), 0.01),
        'rel_bias': nrm((REL_BUCKETS, A_HEADS), 0.5),
        'w_ada': nrm((DEPTH, D_MODEL, 9 * D_MODEL), 0.1 * D_MODEL ** -0.5),
        'b_ada': nrm((DEPTH, 9 * D_MODEL), 0.01),
        'ln_g': 1.0 + nrm((DEPTH, 3, D_MODEL), 0.01),
        'ln_b': nrm((DEPTH, 3, D_MODEL), 0.01),
        'w_ffn_in': nrm((DEPTH, 2, D_MODEL, 2 * D_FF), D_MODEL ** -0.5),
        'w_ffn_out': nrm((DEPTH, 2, D_FF, D_MODEL), BETA * D_FF ** -0.5),
        'w_in': nrm((DEPTH, D_MODEL, D_IN), D_MODEL ** -0.5),
        'w_out': nrm((DEPTH, D_MODEL, D_MODEL), BETA * D_MODEL ** -0.5),
        'g_qnorm': 1.0 + nrm((DEPTH, Q_RANK), 0.01),
        'g_kvnorm': 1.0 + nrm((DEPTH, KV_RANK), 0.01),
        'w_uq': nrm((DEPTH, Q_RANK, B_HEADS, NOPE_DIM + ROPE_DIM), Q_RANK ** -0.5),
        'w_uk': nrm((DEPTH, KV_RANK, B_HEADS, NOPE_DIM), KV_RANK ** -0.5),
        'w_uv': nrm((DEPTH, KV_RANK, B_HEADS, V_DIM), KV_RANK ** -0.5),
        'ssm_lam_re': -0.5 + nrm((DEPTH, C_GROUPS, C_STATE), 0.01),
        'ssm_lam_im': lam_im + nrm((DEPTH, C_GROUPS, C_STATE), 0.01),
        'ssm_log_dt': jax.random.uniform(next(ks), (DEPTH, C_GROUPS), f32, math.log(1e-3), math.log(1e-1)),
        'ssm_b_re': nrm((DEPTH, C_GROUPS, C_STATE, C_GROUP), (2 * C_GROUP) ** -0.5),
        'ssm_b_im': nrm((DEPTH, C_GROUPS, C_STATE, C_GROUP), (2 * C_GROUP) ** -0.5),
        'ssm_c_re': nrm((DEPTH, C_GROUPS, C_GROUP, C_STATE), (2 * C_STATE) ** -0.5),
        'ssm_c_im': nrm((DEPTH, C_GROUPS, C_GROUP, C_STATE), (2 * C_STATE) ** -0.5),
        'ssm_d': nrm((DEPTH, C_GROUPS, C_GROUP)),
        'w_glu': nrm((DEPTH, C_WIDTH, C_WIDTH), C_WIDTH ** -0.5),
        'b_glu': nrm((DEPTH, C_WIDTH), 0.01),
    }


def reference(x_prompt, x_sample, c_prompt, c_sample, page_table, cache_attn_k, cache_attn_v, cache_idx_k, cache_mla_latent, cache_mla_rope, state_ssm_re, state_ssm_im, ln_in_g, ln_in_b, rel_bias, w_ada, b_ada, ln_g, ln_b, w_ffn_in, w_ffn_out, w_in, w_out, g_qnorm, g_kvnorm, w_uq, w_uk, w_uv, ssm_lam_re, ssm_lam_im, ssm_log_dt, ssm_b_re, ssm_b_im, ssm_c_re, ssm_c_im, ssm_d, w_glu, b_glu):

    def ssm_layer(u, s0_re, s0_im, l):
        return s5_mix(u, s0_re, s0_im, ssm_lam_re[l], ssm_lam_im[l], ssm_log_dt[l], ssm_b_re[l], ssm_b_im[l], ssm_c_re[l], ssm_c_im[l], ssm_d[l], w_glu[l], b_glu[l])

    def mla_out(o_lat, l):
        o = jnp.einsum('bthr,rhv->bthv', o_lat, w_uv[l])
        return o.reshape(o.shape[0], o.shape[1], B_WIDTH)

    def prompt_mix(h, l):
        b_ = h.shape[0]
        pos = jnp.arange(SEQ, dtype=jnp.int32)
        a_q, a_k, a_v, i_q, i_w, i_k, q_lat, q_pe, ckv, kpe, u = project_groups(h, pos, w_in[l], g_qnorm[l], g_kvnorm[l], w_uq[l], w_uk[l])
        topk = min(TOPK_MAX, SEQ // 4)
        n_blocks = SEQ // Q_BLOCK

        def to_blocks(t):
            return jnp.swapaxes(t.reshape(b_, n_blocks, Q_BLOCK, *t.shape[2:]), 0, 1)

        def from_blocks(t):
            return jnp.swapaxes(t, 0, 1).reshape(b_, SEQ, *t.shape[3:])

        def attend_block(blk):
            qa, qi, wi, ql, qp, qpos = blk
            sel = indexer_topk(qi, wi, i_k, qpos, pos, topk)
            oa = sparse_attend(qa, gather_rows(a_k, sel), gather_rows(a_v, sel), sel, qpos, rel_bias)
            ob = mla_attend(ql, qp, ckv, kpe, qpos, pos)
            return oa, ob

        oa, ob = lax.map(attend_block, (to_blocks(a_q), to_blocks(i_q), to_blocks(i_w), to_blocks(q_lat), to_blocks(q_pe), pos.reshape(n_blocks, Q_BLOCK)))
        zeros = jnp.zeros((b_, C_GROUPS, C_STATE), jnp.float32)
        y_c, s_re, s_im = ssm_layer(u, zeros, zeros, l)
        mixed = jnp.concatenate([from_blocks(oa), mla_out(from_blocks(ob), l), y_c], axis=-1)
        return mixed, (a_k, a_v, i_k, ckv, kpe, s_re, s_im)

    def sample_mix(h, l):
        n_dec = h.shape[1]
        pos = PAST_LEN + jnp.arange(n_dec, dtype=jnp.int32)
        k_pos = jnp.arange(PAST_LEN + n_dec, dtype=jnp.int32)
        a_q, a_k, a_v, i_q, i_w, i_k, q_lat, q_pe, ckv, kpe, u = project_groups(h, pos, w_in[l], g_qnorm[l], g_kvnorm[l], w_uq[l], w_uk[l])
        topk = min(TOPK_MAX, (PAST_LEN + n_dec) // 4)
        i_k_all = jnp.concatenate([gather_paged_all(cache_idx_k, l, page_table), i_k], axis=1)
        sel = indexer_topk(i_q, i_w, i_k_all, pos, k_pos, topk)
        is_past = sel < PAST_LEN
        sel_past = jnp.minimum(sel, PAST_LEN - 1)
        sel_new = jnp.clip(sel - PAST_LEN, 0, n_dec - 1)
        k_sel = jnp.where(is_past[..., None, None], gather_paged_rows(cache_attn_k, l, page_table, sel_past), gather_rows(a_k, sel_new))
        v_sel = jnp.where(is_past[..., None, None], gather_paged_rows(cache_attn_v, l, page_table, sel_past), gather_rows(a_v, sel_new))
        oa = sparse_attend(a_q, k_sel, v_sel, sel, pos, rel_bias)
        ckv_all = jnp.concatenate([gather_paged_all(cache_mla_latent, l, page_table), ckv], axis=1)
        kpe_all = jnp.concatenate([gather_paged_all(cache_mla_rope, l, page_table), kpe], axis=1)
        ob = mla_attend(q_lat, q_pe, ckv_all, kpe_all, pos, k_pos)
        y_c, s_re, s_im = ssm_layer(u, state_ssm_re[l], state_ssm_im[l], l)
        mixed = jnp.concatenate([oa, mla_out(ob, l), y_c], axis=-1)
        return mixed, (a_k, a_v, i_k, ckv, kpe, s_re, s_im)

    y_prompt, st_p = run_trunk(x_prompt, c_prompt, prompt_mix, ln_in_g, ln_in_b, w_ada, b_ada, ln_g, ln_b, w_ffn_in, w_ffn_out, w_out)
    y_sample, st_s = run_trunk(x_sample, c_sample, sample_mix, ln_in_g, ln_in_b, w_ada, b_ada, ln_g, ln_b, w_ffn_in, w_ffn_out, w_out)
    return (y_prompt, y_sample, st_p[0], st_s[0], st_p[1], st_s[1], st_p[2], st_s[2], st_p[3], st_s[3], st_p[4], st_s[4], st_p[5], st_s[5], st_p[6], st_s[6])
```

```python
import functools
import math

import jax
import jax.numpy as jnp
import numpy as np
from jax import lax
from jax.experimental import pallas as pl
from jax.experimental.pallas import tpu as pltpu

D_MODEL = 1024
BATCH = 2
SEQ = 8192
DEPTH = 2
DEC_BATCH = 128
DEC_SEQ = 1
PAST_LEN = 16384
PAGE_SIZE = 128
HEAD_DIM = 64
A_HEADS = 4
A_KV_HEADS = 2
IDX_HEADS = 4
IDX_DIM = 32
TOPK_MAX = 256
REL_BUCKETS = 32
REL_MAX_DIST = 128
B_HEADS = 4
Q_RANK = 256
KV_RANK = 128
NOPE_DIM = 64
ROPE_DIM = 32
V_DIM = 64
ROPE_THETA = 10000.0
A_WIDTH = A_HEADS * HEAD_DIM
B_WIDTH = B_HEADS * V_DIM
C_WIDTH = D_MODEL - A_WIDTH - B_WIDTH
C_GROUP = 16
C_GROUPS = C_WIDTH // C_GROUP
C_STATE = 64
D_FF = 2816
Q_BLOCK = 128
LN_EPS = 1e-5
ALPHA = (2.0 * DEPTH) ** 0.25
MLA_SCALE = (NOPE_DIM + ROPE_DIM) ** -0.5
IN_SPLITS = (A_HEADS * HEAD_DIM, A_KV_HEADS * HEAD_DIM, A_KV_HEADS * HEAD_DIM, IDX_HEADS * IDX_DIM, IDX_HEADS, IDX_DIM, Q_RANK, KV_RANK, ROPE_DIM, C_WIDTH)
D_IN = sum(IN_SPLITS)

VMEM_LIMIT_V7X = 56 * 1024 * 1024
FF_CHUNK = 256
N_FF_CHUNKS = D_FF // FF_CHUNK
FFN_ROWS = 512


def _ln(v, g, b):
    mu = jnp.mean(v, -1, keepdims=True)
    d = v - mu
    var = jnp.mean(d * d, -1, keepdims=True)
    return d * lax.rsqrt(var + LN_EPS) * g + b


def _ffn_kernel(pre_ln, x_ref, shift_ref, scale_ref, gate_ref, wg_ref, wu_ref, wd_ref, lng_ref, lnb_ref, *rest):
    if pre_ln:
        ing_ref, inb_ref, o_ref, acc_ref = rest
    else:
        o_ref, acc_ref = rest
    x = x_ref[...]
    if pre_ln:
        x = _ln(x, ing_ref[...], inb_ref[...])
    h = (x * (1.0 + scale_ref[0]) + shift_ref[0]).astype(jnp.bfloat16)
    acc_ref[...] = jnp.zeros_like(acc_ref)

    def chunk(c, carry):
        g = jnp.dot(h, wg_ref[c], preferred_element_type=jnp.float32)
        u = jnp.dot(h, wu_ref[c], preferred_element_type=jnp.float32)
        act = (g * jax.nn.sigmoid(g) * u).astype(jnp.bfloat16)
        acc_ref[...] += jnp.dot(act, wd_ref[c], preferred_element_type=jnp.float32)
        return carry

    lax.fori_loop(0, N_FF_CHUNKS, chunk, 0)
    y = ALPHA * x + 0.5 * (1.0 + gate_ref[0]) * acc_ref[...]
    o_ref[...] = _ln(y, lng_ref[...], lnb_ref[...])


def _mod_spec(mod, rows, rows_per_group):
    r = mod.shape[1]
    if r == 1:
        return pl.BlockSpec((1, 1, D_MODEL), lambda i: (i // (rows_per_group // rows), 0, 0))
    return pl.BlockSpec((1, rows, D_MODEL), lambda i: (0, i, 0))


def _const_spec(shape):
    nd = len(shape)
    return pl.BlockSpec(shape, lambda i: (0,) * nd, pipeline_mode=pl.Buffered(1))


def ffn_block(x, shift, scale, gate, wg, wu, wd, ln_g, ln_b, rows_per_group, ln_in=None):
    n = x.shape[0]
    rows = min(FFN_ROWS, n)
    pre_ln = ln_in is not None
    row_spec = pl.BlockSpec((rows, D_MODEL), lambda i: (i, 0))
    vec_spec = _const_spec((1, D_MODEL))
    in_specs = [row_spec] + [_mod_spec(m, rows, rows_per_group) for m in (shift, scale, gate)] + [
        _const_spec(wg.shape), _const_spec(wu.shape), _const_spec(wd.shape), vec_spec, vec_spec]
    args = [x, shift, scale, gate, wg, wu, wd, ln_g.reshape(1, D_MODEL), ln_b.reshape(1, D_MODEL)]
    if pre_ln:
        in_specs += [vec_spec, vec_spec]
        args += [ln_in[0].reshape(1, D_MODEL), ln_in[1].reshape(1, D_MODEL)]
    return pl.pallas_call(
        functools.partial(_ffn_kernel, pre_ln),
        grid=(n // rows,),
        in_specs=in_specs,
        out_specs=row_spec,
        out_shape=jax.ShapeDtypeStruct((n, D_MODEL), jnp.float32),
        scratch_shapes=[pltpu.VMEM((rows, D_MODEL), jnp.float32)],
        compiler_params=pltpu.CompilerParams(dimension_semantics=("arbitrary",), vmem_limit_bytes=VMEM_LIMIT_V7X),
        name="ffn_block",
    )(*args)


def _ffn_weights(w_in, w_out):
    wg = w_in[:, :D_FF].reshape(D_MODEL, N_FF_CHUNKS, FF_CHUNK).transpose(1, 0, 2).astype(jnp.bfloat16)
    wu = w_in[:, D_FF:].reshape(D_MODEL, N_FF_CHUNKS, FF_CHUNK).transpose(1, 0, 2).astype(jnp.bfloat16)
    wd = w_out.reshape(N_FF_CHUNKS, FF_CHUNK, D_MODEL).astype(jnp.bfloat16)
    return wg, wu, wd


def _layer_norm(x, g, b):
    xf = x.astype(jnp.float32)
    mu = jnp.mean(xf, -1, keepdims=True)
    var = jnp.mean(jnp.square(xf - mu), -1, keepdims=True)
    return ((xf - mu) * lax.rsqrt(var + LN_EPS) * g + b).astype(x.dtype)


def _rms_norm(x, g):
    xf = x.astype(jnp.float32)
    return (xf * lax.rsqrt(jnp.mean(xf * xf, -1, keepdims=True) + 1e-6) * g).astype(x.dtype)


def _rope(x, pos):
    half = ROPE_DIM // 2
    freqs = ROPE_THETA ** (-jnp.arange(half, dtype=jnp.float32) / half)
    ang = pos.astype(jnp.float32)[:, None] * freqs
    cos = jnp.cos(ang)[:, None, :]
    sin = jnp.sin(ang)[:, None, :]
    x1 = x[..., :half].astype(jnp.float32)
    x2 = x[..., half:].astype(jnp.float32)
    return jnp.concatenate([x1 * cos - x2 * sin, x1 * sin + x2 * cos], -1).astype(x.dtype)


def _t5_bucket(dist):
    n = jnp.maximum(dist, 0)
    max_exact = REL_BUCKETS // 2
    n_f = jnp.maximum(n, 1).astype(jnp.float32)
    log_bucket = max_exact + (jnp.log(n_f / max_exact) / math.log(REL_MAX_DIST / max_exact) * (REL_BUCKETS - max_exact)).astype(jnp.int32)
    return jnp.where(n < max_exact, n, jnp.minimum(log_bucket, REL_BUCKETS - 1))


def _gather_rows(rows, sel):
    return jax.vmap(lambda r, i: r[i])(rows, sel)


def _gather_paged_all(pool, l, page_table):
    g = pool[l, page_table]
    return g.reshape(g.shape[0], g.shape[1] * g.shape[2], *g.shape[3:])


def _gather_paged_rows(pool, l, page_table, pos):
    db = page_table.shape[0]
    phys = jnp.take_along_axis(page_table, (pos // PAGE_SIZE).reshape(db, -1), axis=1).reshape(pos.shape)
    return pool[l, phys, pos % PAGE_SIZE]


def _project_groups(h, pos, w_in, g_qn, g_kvn, w_uq, w_uk):
    b_, t_ = h.shape[0], h.shape[1]
    p = jnp.einsum('btd,de->bte', h, w_in)
    split_at = [int(s) for s in np.cumsum(IN_SPLITS)[:-1]]
    a_q, a_k, a_v, i_q, i_w, i_k, b_cq, b_ckv, b_kpe, c_u = jnp.split(p, split_at, axis=-1)
    a_q = a_q.reshape(b_, t_, A_HEADS, HEAD_DIM)
    a_k = a_k.reshape(b_, t_, A_KV_HEADS, HEAD_DIM)
    a_v = a_v.reshape(b_, t_, A_KV_HEADS, HEAD_DIM)
    i_q = i_q.reshape(b_, t_, IDX_HEADS, IDX_DIM) * (IDX_DIM ** -0.5)
    i_w = i_w * (IDX_HEADS ** -0.5)
    cq = _rms_norm(b_cq, g_qn)
    q = jnp.einsum('btr,rhe->bthe', cq, w_uq)
    q_pe = _rope(q[..., NOPE_DIM:], pos)
    q_lat = jnp.einsum('bthn,rhn->bthr', q[..., :NOPE_DIM], w_uk)
    ckv = _rms_norm(b_ckv, g_kvn)
    kpe = _rope(b_kpe[:, :, None, :], pos)[:, :, 0]
    return a_q, a_k, a_v, i_q, i_w, i_k, q_lat, q_pe, ckv, kpe, c_u


def _indexer_topk(q_idx, w_idx, k_idx, q_pos, k_pos, topk):
    s = jax.nn.relu(jnp.einsum('bthd,bsd->bths', q_idx, k_idx).astype(jnp.float32))
    score = jnp.einsum('bths,bth->bts', s, w_idx.astype(jnp.float32))
    score = jnp.where(k_pos[None, None, :] <= q_pos[None, :, None], score, -jnp.inf)
    _, sel = lax.top_k(score, topk)
    return sel


def _sparse_attend(q, k_sel, v_sel, sel, q_pos, rel_bias):
    b_, t_, n_sel = sel.shape
    grp = A_HEADS // A_KV_HEADS
    qg = q.reshape(b_, t_, A_KV_HEADS, grp, HEAD_DIM)
    logits = jnp.einsum('btcgd,btncd->btcgn', qg, k_sel).astype(jnp.float32) * (HEAD_DIM ** -0.5)
    dist = q_pos[None, :, None] - sel
    bias = rel_bias[_t5_bucket(dist)].astype(jnp.float32)
    bias = bias.reshape(b_, t_, n_sel, A_KV_HEADS, grp).transpose(0, 1, 3, 4, 2)
    logits = jnp.where((dist >= 0)[:, :, None, None, :], logits + bias, -jnp.inf)
    p = jax.nn.softmax(logits, axis=-1).astype(v_sel.dtype)
    o = jnp.einsum('btcgn,btncd->btcgd', p, v_sel)
    return o.reshape(b_, t_, A_WIDTH)


def _mla_attend(q_lat, q_pe, ckv, kpe, q_pos, k_pos):
    logits = (jnp.einsum('bthr,bsr->bhts', q_lat, ckv) + jnp.einsum('bthe,bse->bhts', q_pe, kpe)).astype(jnp.float32) * MLA_SCALE
    logits = jnp.where(k_pos[None, None, None, :] <= q_pos[None, None, :, None], logits, -jnp.inf)
    p = jax.nn.softmax(logits, axis=-1).astype(ckv.dtype)
    return jnp.einsum('bhts,bsr->bthr', p, ckv)


def _s5_mix(u, s0_re, s0_im, lam_re, lam_im, log_dt, b_re, b_im, c_re, c_im, d_skip, w_glu, b_glu):
    f32 = jnp.float32
    b_, t_ = u.shape[0], u.shape[1]
    lam = lax.complex(lam_re.astype(f32), lam_im.astype(f32))
    dt = jnp.exp(log_dt.astype(f32))[:, None]
    a_bar = jnp.exp(lam * dt)
    b_bar = ((a_bar - 1.0) / lam)[..., None] * lax.complex(b_re.astype(f32), b_im.astype(f32))
    c_mat = lax.complex(c_re.astype(f32), c_im.astype(f32))
    ug = u.astype(f32).reshape(b_, t_, C_GROUPS, C_GROUP)
    bu = jnp.einsum('gpc,btgc->btgp', b_bar, ug.astype(jnp.complex64))
    a_seq = jnp.broadcast_to(a_bar, bu.shape)

    def combine(e1, e2):
        a1, v1 = e1
        a2, v2 = e2
        return a2 * a1, a2 * v1 + v2

    a_cum, s = lax.associative_scan(combine, (a_seq, bu), axis=1)
    s0 = lax.complex(s0_re.astype(f32), s0_im.astype(f32))
    s = s + a_cum * s0[:, None]
    y = jnp.einsum('gcp,btgp->btgc', c_mat, s).real + d_skip.astype(f32) * ug
    z = jax.nn.gelu(y.reshape(b_, t_, C_WIDTH))
    out = z * jax.nn.sigmoid(jnp.einsum('bte,ef->btf', z, w_glu) + b_glu)
    s_last = s[:, -1]
    return out.astype(u.dtype), jnp.real(s_last), jnp.imag(s_last)


def _run_trunk(x, c, mix_fn, ln_in_g, ln_in_b, w_ada, b_ada, ln_g, ln_b, ffn_w, w_out):
    b_, t_ = x.shape[0], x.shape[1]
    per_token = t_ == 1
    xf = x.reshape(b_ * t_, D_MODEL)
    states = []
    for l in range(DEPTH):
        mod = jnp.einsum('bd,de->be', jax.nn.silu(c), w_ada[l]) + b_ada[l]
        mod = mod.reshape(b_, 3, 3, D_MODEL)

        def mod_of(j, k):
            m = mod[:, j, k]
            return m.reshape(1, b_, D_MODEL) if per_token else m.reshape(b_, 1, D_MODEL)

        xf = ffn_block(xf, mod_of(0, 0), mod_of(0, 1), mod_of(0, 2), *ffn_w[l][0], ln_g[l, 0], ln_b[l, 0], t_,
                       ln_in=(ln_in_g, ln_in_b) if l == 0 else None)
        xx = xf.reshape(b_, t_, D_MODEL)
        mixed, st = mix_fn(xx * (1.0 + mod[:, 1, 1][:, None]) + mod[:, 1, 0][:, None], l)
        y = jnp.einsum('bte,ed->btd', mixed, w_out[l])
        xx = _layer_norm(ALPHA * xx + (1.0 + mod[:, 1, 2][:, None]) * y, ln_g[l, 1], ln_b[l, 1])
        xf = xx.reshape(b_ * t_, D_MODEL)
        xf = ffn_block(xf, mod_of(2, 0), mod_of(2, 1), mod_of(2, 2), *ffn_w[l][1], ln_g[l, 2], ln_b[l, 2], t_)
        states.append(st)
    stacked = [jnp.stack(s, axis=0) for s in zip(*states)]
    return xf.reshape(b_, t_, D_MODEL), stacked


def kernel(x_prompt, x_sample, c_prompt, c_sample, page_table, cache_attn_k, cache_attn_v, cache_idx_k, cache_mla_latent, cache_mla_rope, state_ssm_re, state_ssm_im, ln_in_g, ln_in_b, rel_bias, w_ada, b_ada, ln_g, ln_b, w_ffn_in, w_ffn_out, w_in, w_out, g_qnorm, g_kvnorm, w_uq, w_uk, w_uv, ssm_lam_re, ssm_lam_im, ssm_log_dt, ssm_b_re, ssm_b_im, ssm_c_re, ssm_c_im, ssm_d, w_glu, b_glu):

    ffn_w = [[_ffn_weights(w_ffn_in[l, j], w_ffn_out[l, j]) for j in range(2)] for l in range(DEPTH)]

    def ssm_layer(u, s0_re, s0_im, l):
        return _s5_mix(u, s0_re, s0_im, ssm_lam_re[l], ssm_lam_im[l], ssm_log_dt[l], ssm_b_re[l], ssm_b_im[l], ssm_c_re[l], ssm_c_im[l], ssm_d[l], w_glu[l], b_glu[l])

    def mla_out(o_lat, l):
        o = jnp.einsum('bthr,rhv->bthv', o_lat, w_uv[l])
        return o.reshape(o.shape[0], o.shape[1], B_WIDTH)

    def prompt_mix(h, l):
        b_ = h.shape[0]
        pos = jnp.arange(SEQ, dtype=jnp.int32)
        a_q, a_k, a_v, i_q, i_w, i_k, q_lat, q_pe, ckv, kpe, u = _project_groups(h, pos, w_in[l], g_qnorm[l], g_kvnorm[l], w_uq[l], w_uk[l])
        topk = min(TOPK_MAX, SEQ // 4)
        n_blocks = SEQ // Q_BLOCK

        def to_blocks(t):
            return jnp.swapaxes(t.reshape(b_, n_blocks, Q_BLOCK, *t.shape[2:]), 0, 1)

        def from_blocks(t):
            return jnp.swapaxes(t, 0, 1).reshape(b_, SEQ, *t.shape[3:])

        def attend_block(blk):
            qa, qi, wi, ql, qp, qpos = blk
            sel = _indexer_topk(qi, wi, i_k, qpos, pos, topk)
            oa = _sparse_attend(qa, _gather_rows(a_k, sel), _gather_rows(a_v, sel), sel, qpos, rel_bias)
            ob = _mla_attend(ql, qp, ckv, kpe, qpos, pos)
            return oa, ob

        oa, ob = lax.map(attend_block, (to_blocks(a_q), to_blocks(i_q), to_blocks(i_w), to_blocks(q_lat), to_blocks(q_pe), pos.reshape(n_blocks, Q_BLOCK)))
        zeros = jnp.zeros((b_, C_GROUPS, C_STATE), jnp.float32)
        y_c, s_re, s_im = ssm_layer(u, zeros, zeros, l)
        mixed = jnp.concatenate([from_blocks(oa), mla_out(from_blocks(ob), l), y_c], axis=-1)
        return mixed, (a_k, a_v, i_k, ckv, kpe, s_re, s_im)

    def sample_mix(h, l):
        n_dec = h.shape[1]
        pos = PAST_LEN + jnp.arange(n_dec, dtype=jnp.int32)
        k_pos = jnp.arange(PAST_LEN + n_dec, dtype=jnp.int32)
        a_q, a_k, a_v, i_q, i_w, i_k, q_lat, q_pe, ckv, kpe, u = _project_groups(h, pos, w_in[l], g_qnorm[l], g_kvnorm[l], w_uq[l], w_uk[l])
        topk = min(TOPK_MAX, (PAST_LEN + n_dec) // 4)
        i_k_all = jnp.concatenate([_gather_paged_all(cache_idx_k, l, page_table), i_k], axis=1)
        sel = _indexer_topk(i_q, i_w, i_k_all, pos, k_pos, topk)
        is_past = sel < PAST_LEN
        sel_past = jnp.minimum(sel, PAST_LEN - 1)
        sel_new = jnp.clip(sel - PAST_LEN, 0, n_dec - 1)
        k_sel = jnp.where(is_past[..., None, None], _gather_paged_rows(cache_attn_k, l, page_table, sel_past), _gather_rows(a_k, sel_new))
        v_sel = jnp.where(is_past[..., None, None], _gather_paged_rows(cache_attn_v, l, page_table, sel_past), _gather_rows(a_v, sel_new))
        oa = _sparse_attend(a_q, k_sel, v_sel, sel, pos, rel_bias)
        ckv_all = jnp.concatenate([_gather_paged_all(cache_mla_latent, l, page_table), ckv], axis=1)
        kpe_all = jnp.concatenate([_gather_paged_all(cache_mla_rope, l, page_table), kpe], axis=1)
        ob = _mla_attend(q_lat, q_pe, ckv_all, kpe_all, pos, k_pos)
        y_c, s_re, s_im = ssm_layer(u, state_ssm_re[l], state_ssm_im[l], l)
        mixed = jnp.concatenate([oa, mla_out(ob, l), y_c], axis=-1)
        return mixed, (a_k, a_v, i_k, ckv, kpe, s_re, s_im)

    y_prompt, st_p = _run_trunk(x_prompt, c_prompt, prompt_mix, ln_in_g, ln_in_b, w_ada, b_ada, ln_g, ln_b, ffn_w, w_out)
    y_sample, st_s = _run_trunk(x_sample, c_sample, sample_mix, ln_in_g, ln_in_b, w_ada, b_ada, ln_g, ln_b, ffn_w, w_out)
    return (y_prompt, y_sample, st_p[0], st_s[0], st_p[1], st_s[1], st_p[2], st_s[2], st_p[3], st_s[3], st_p[4], st_s[4], st_p[5], st_s[5], st_p[6], st_s[6])
```

```python
import functools
import math

import jax
import jax.numpy as jnp
import numpy as np
from jax import lax
from jax.experimental import pallas as pl
from jax.experimental.pallas import tpu as pltpu

D_MODEL = 1024
BATCH = 2
SEQ = 8192
DEPTH = 2
DEC_BATCH = 128
DEC_SEQ = 1
PAST_LEN = 16384
PAGE_SIZE = 128
HEAD_DIM = 64
A_HEADS = 4
A_KV_HEADS = 2
IDX_HEADS = 4
IDX_DIM = 32
TOPK_MAX = 256
REL_BUCKETS = 32
REL_MAX_DIST = 128
B_HEADS = 4
Q_RANK = 256
KV_RANK = 128
NOPE_DIM = 64
ROPE_DIM = 32
V_DIM = 64
ROPE_THETA = 10000.0
A_WIDTH = A_HEADS * HEAD_DIM
B_WIDTH = B_HEADS * V_DIM
C_WIDTH = D_MODEL - A_WIDTH - B_WIDTH
C_GROUP = 16
C_GROUPS = C_WIDTH // C_GROUP
C_STATE = 64
D_FF = 2816
Q_BLOCK = 128
LN_EPS = 1e-5
ALPHA = (2.0 * DEPTH) ** 0.25
MLA_SCALE = (NOPE_DIM + ROPE_DIM) ** -0.5
IN_SPLITS = (A_HEADS * HEAD_DIM, A_KV_HEADS * HEAD_DIM, A_KV_HEADS * HEAD_DIM, IDX_HEADS * IDX_DIM, IDX_HEADS, IDX_DIM, Q_RANK, KV_RANK, ROPE_DIM, C_WIDTH)
D_IN = sum(IN_SPLITS)

VMEM_LIMIT_V7X = 56 * 1024 * 1024
LANES = 128
FF_CHUNK = 256
N_FF_CHUNKS = D_FF // FF_CHUNK
ROWS = 512
TQ = 128
TK = 512
S5_CHUNK = 16
INT_MIN = -2 ** 31

COL_AQ = 0
COL_AK = COL_AQ + A_HEADS * LANES
COL_AV = COL_AK + LANES
COL_IQ = COL_AV + LANES
COL_CQ = COL_IQ + IDX_HEADS * LANES
COL_CKV = COL_CQ + Q_RANK
COL_CU = COL_CKV + KV_RANK
COL_KA = COL_CU + C_WIDTH
COL_KB = COL_KA + LANES
COL_MISC = COL_KB + LANES
N_PROJ = COL_MISC + LANES
UQ_NOPE = 0
UQ_A = B_HEADS * NOPE_DIM
UQ_B = UQ_A + B_HEADS * LANES
N_UQ = UQ_B + B_HEADS * LANES


def _params(n_axes=1):
    return pltpu.CompilerParams(dimension_semantics=("arbitrary",) * n_axes, vmem_limit_bytes=VMEM_LIMIT_V7X)


def _ln(v, g, b):
    mu = jnp.mean(v, -1, keepdims=True)
    d = v - mu
    var = jnp.mean(d * d, -1, keepdims=True)
    return d * lax.rsqrt(var + LN_EPS) * g + b


def _mod_spec(mod, rows, rows_per_group):
    if mod.shape[1] == 1:
        return pl.BlockSpec((1, 1, D_MODEL), lambda i: (i // (rows_per_group // rows), 0, 0))
    return pl.BlockSpec((1, rows, D_MODEL), lambda i: (0, i, 0))


def _const_spec(shape):
    nd = len(shape)
    return pl.BlockSpec(shape, lambda *_: (0,) * nd, pipeline_mode=pl.Buffered(1))


ADA_COLS = 1152


def _adaln_kernel(c_ref, w_ref, b_ref, o_ref):
    c = c_ref[...]
    o_ref[0] = jnp.dot(c * jax.nn.sigmoid(c), w_ref[0], preferred_element_type=jnp.float32) + b_ref[0]


def adaln(c, w_ada, b_ada):
    n = c.shape[0]
    n_out = w_ada.shape[2]
    return pl.pallas_call(
        _adaln_kernel,
        grid=(DEPTH, n_out // ADA_COLS),
        in_specs=[pl.BlockSpec((n, D_MODEL), lambda l, j: (0, 0)),
                  pl.BlockSpec((1, D_MODEL, ADA_COLS), lambda l, j: (l, 0, j)),
                  pl.BlockSpec((1, 1, ADA_COLS), lambda l, j: (l, 0, j))],
        out_specs=pl.BlockSpec((1, n, ADA_COLS), lambda l, j: (l, 0, j)),
        out_shape=jax.ShapeDtypeStruct((DEPTH, n, n_out), jnp.float32),
        compiler_params=_params(2),
        name="adaln",
    )(c, w_ada, b_ada.reshape(DEPTH, 1, n_out))


def _ffn_kernel(pre_ln, x_ref, shift_ref, scale_ref, gate_ref, wg_ref, wu_ref, wd_ref, lng_ref, lnb_ref, *rest):
    if pre_ln:
        ing_ref, inb_ref, o_ref, acc_ref = rest
    else:
        o_ref, acc_ref = rest
    x = x_ref[...]
    if pre_ln:
        x = _ln(x, ing_ref[...], inb_ref[...])
    h = (x * (1.0 + scale_ref[0]) + shift_ref[0]).astype(jnp.bfloat16)
    acc_ref[...] = jnp.zeros_like(acc_ref)

    def chunk(c, carry):
        g = jnp.dot(h, wg_ref[c], preferred_element_type=jnp.float32)
        u = jnp.dot(h, wu_ref[c], preferred_element_type=jnp.float32)
        act = (g * jax.nn.sigmoid(g) * u).astype(jnp.bfloat16)
        acc_ref[...] += jnp.dot(act, wd_ref[c], preferred_element_type=jnp.float32)
        return carry

    lax.fori_loop(0, N_FF_CHUNKS, chunk, 0)
    y = ALPHA * x + 0.5 * (1.0 + gate_ref[0]) * acc_ref[...]
    o_ref[...] = _ln(y, lng_ref[...], lnb_ref[...])


def ffn_block(x, shift, scale, gate, wg, wu, wd, ln_g, ln_b, rows_per_group, ln_in=None):
    n = x.shape[0]
    rows = min(ROWS, n)
    pre_ln = ln_in is not None
    row_spec = pl.BlockSpec((rows, D_MODEL), lambda i: (i, 0))
    vec_spec = _const_spec((1, D_MODEL))
    in_specs = [row_spec] + [_mod_spec(m, rows, rows_per_group) for m in (shift, scale, gate)] + [
        _const_spec(wg.shape), _const_spec(wu.shape), _const_spec(wd.shape), vec_spec, vec_spec]
    args = [x, shift, scale, gate, wg, wu, wd, ln_g.reshape(1, D_MODEL), ln_b.reshape(1, D_MODEL)]
    if pre_ln:
        in_specs += [vec_spec, vec_spec]
        args += [ln_in[0].reshape(1, D_MODEL), ln_in[1].reshape(1, D_MODEL)]
    return pl.pallas_call(
        functools.partial(_ffn_kernel, pre_ln),
        grid=(n // rows,),
        in_specs=in_specs,
        out_specs=row_spec,
        out_shape=jax.ShapeDtypeStruct((n, D_MODEL), jnp.float32),
        scratch_shapes=[pltpu.VMEM((rows, D_MODEL), jnp.float32)],
        compiler_params=_params(),
        name="ffn_block",
    )(*args)


def _ffn_weights(w_in, w_out):
    wg = w_in[:, :D_FF].reshape(D_MODEL, N_FF_CHUNKS, FF_CHUNK).transpose(1, 0, 2).astype(jnp.bfloat16)
    wu = w_in[:, D_FF:].reshape(D_MODEL, N_FF_CHUNKS, FF_CHUNK).transpose(1, 0, 2).astype(jnp.bfloat16)
    wd = w_out.reshape(N_FF_CHUNKS, FF_CHUNK, D_MODEL).astype(jnp.bfloat16)
    return wg, wu, wd


def _mixproj_kernel(x_ref, shift_ref, scale_ref, wp_ref, gq_ref, gkv_ref, wuq_ref, wuk_ref, cos_ref, sin_ref,
                    aq_ref, ak_ref, av_ref, kv_ref, iq_ref, misc_ref, ik_ref, ckv_ref, kpe_ref, kmla_ref, qmla_ref, cu_ref):
    bf16 = jnp.bfloat16
    h = (x_ref[...] * (1.0 + scale_ref[0]) + shift_ref[0]).astype(bf16)
    p = jnp.dot(h, wp_ref[...], preferred_element_type=jnp.float32)
    aq_ref[...] = (p[:, COL_AQ:COL_AK] * (HEAD_DIM ** -0.5)).astype(bf16)
    ak = p[:, COL_AK:COL_AV]
    av = p[:, COL_AV:COL_IQ]
    ak_ref[...] = ak
    av_ref[...] = av
    kv_ref[...] = p[:, COL_AK:COL_IQ].astype(bf16)
    iq_ref[...] = (p[:, COL_IQ:COL_CQ] * (IDX_DIM ** -0.5)).astype(bf16)
    lane = lax.broadcasted_iota(jnp.int32, (1, LANES), 1)
    is_w = (lane >= IDX_DIM) & (lane < IDX_DIM + IDX_HEADS)
    misc = p[:, COL_MISC:N_PROJ] * jnp.where(is_w, IDX_HEADS ** -0.5, 1.0)
    misc_ref[...] = misc
    ik_ref[...] = misc.astype(bf16)
    cos = cos_ref[...]
    sin = sin_ref[...]
    kpe = p[:, COL_KA:COL_KB] * cos + p[:, COL_KB:COL_MISC] * sin
    kpe_ref[...] = kpe
    ckv_in = p[:, COL_CKV:COL_CU]
    ckv = ckv_in * lax.rsqrt(jnp.mean(ckv_in * ckv_in, -1, keepdims=True) + 1e-6) * gkv_ref[...]
    ckv_ref[...] = ckv
    kmla_ref[...] = jnp.concatenate([ckv, kpe], axis=1).astype(bf16)
    cq_in = p[:, COL_CQ:COL_CKV]
    cq = (cq_in * lax.rsqrt(jnp.mean(cq_in * cq_in, -1, keepdims=True) + 1e-6) * gq_ref[...]).astype(bf16)
    q = jnp.dot(cq, wuq_ref[...], preferred_element_type=jnp.float32)
    q_lat = jnp.dot(q[:, UQ_NOPE:UQ_A].astype(bf16), wuk_ref[...], preferred_element_type=jnp.float32)
    pieces = []
    for hd in range(B_HEADS):
        q_pe = q[:, UQ_A + hd * LANES:UQ_A + (hd + 1) * LANES] * cos + q[:, UQ_B + hd * LANES:UQ_B + (hd + 1) * LANES] * sin
        pieces += [q_lat[:, hd * KV_RANK:(hd + 1) * KV_RANK], q_pe]
    qmla_ref[...] = jnp.concatenate(pieces, axis=1).astype(bf16)
    cu_ref[...] = p[:, COL_CU:COL_KA]


def mixproj(x, shift, scale, wp, gq, gkv, wuq, wuk, cos, sin, rows_per_group):
    n = x.shape[0]
    rows = min(ROWS, n)
    f32, bf16 = jnp.float32, jnp.bfloat16

    def rspec(w):
        return pl.BlockSpec((rows, w), lambda i: (i, 0))

    out_w = [(A_HEADS * LANES, bf16), (LANES, f32), (LANES, f32), (2 * LANES, bf16), (IDX_HEADS * LANES, bf16),
             (LANES, f32), (LANES, bf16), (KV_RANK, f32), (LANES, f32), (2 * LANES, bf16),
             (B_HEADS * 2 * LANES, bf16), (C_WIDTH, f32)]
    return pl.pallas_call(
        _mixproj_kernel,
        grid=(n // rows,),
        in_specs=[rspec(D_MODEL), _mod_spec(shift, rows, rows_per_group), _mod_spec(scale, rows, rows_per_group),
                  _const_spec(wp.shape), _const_spec((1, Q_RANK)), _const_spec((1, KV_RANK)),
                  _const_spec(wuq.shape), _const_spec(wuk.shape), rspec(LANES), rspec(LANES)],
        out_specs=[rspec(w) for w, _ in out_w],
        out_shape=[jax.ShapeDtypeStruct((n, w), dt) for w, dt in out_w],
        compiler_params=_params(),
        name="mixproj",
    )(x, shift, scale, wp, gq.reshape(1, Q_RANK), gkv.reshape(1, KV_RANK), wuq, wuk, cos, sin)


def _mix_weights(w_in, w_uq, w_uk):
    f32 = jnp.float32
    off = np.concatenate([[0], np.cumsum(IN_SPLITS)])
    a_q, a_k, a_v, i_q, i_w, i_k, b_cq, b_ckv, b_kpe, c_u = [w_in[:, off[i]:off[i + 1]] for i in range(10)]
    wp = jnp.zeros((D_MODEL, N_PROJ), f32)
    grp = A_HEADS // A_KV_HEADS
    for hd in range(A_HEADS):
        lo = COL_AQ + hd * LANES + HEAD_DIM * (hd // grp)
        wp = wp.at[:, lo:lo + HEAD_DIM].set(a_q[:, hd * HEAD_DIM:(hd + 1) * HEAD_DIM])
    wp = wp.at[:, COL_AK:COL_AK + LANES].set(a_k)
    wp = wp.at[:, COL_AV:COL_AV + LANES].set(a_v)
    for hd in range(IDX_HEADS):
        lo = COL_IQ + hd * LANES
        wp = wp.at[:, lo:lo + IDX_DIM].set(i_q[:, hd * IDX_DIM:(hd + 1) * IDX_DIM])
    wp = wp.at[:, COL_CQ:COL_CQ + Q_RANK].set(b_cq)
    wp = wp.at[:, COL_CKV:COL_CKV + KV_RANK].set(b_ckv)
    wp = wp.at[:, COL_CU:COL_CU + C_WIDTH].set(c_u)
    half = ROPE_DIM // 2
    wp = wp.at[:, COL_KA:COL_KA + ROPE_DIM].set(b_kpe)
    wp = wp.at[:, COL_KB:COL_KB + half].set(-b_kpe[:, half:])
    wp = wp.at[:, COL_KB + half:COL_KB + ROPE_DIM].set(b_kpe[:, :half])
    wp = wp.at[:, COL_MISC:COL_MISC + IDX_DIM].set(i_k)
    wp = wp.at[:, COL_MISC + IDX_DIM:COL_MISC + IDX_DIM + IDX_HEADS].set(i_w)
    wuq = jnp.zeros((Q_RANK, N_UQ), f32)
    wuk = jnp.zeros((B_HEADS * NOPE_DIM, B_HEADS * KV_RANK), f32)
    for hd in range(B_HEADS):
        wuq = wuq.at[:, UQ_NOPE + hd * NOPE_DIM:UQ_NOPE + (hd + 1) * NOPE_DIM].set(w_uq[:, hd, :NOPE_DIM])
        r = w_uq[:, hd, NOPE_DIM:]
        wuq = wuq.at[:, UQ_A + hd * LANES:UQ_A + hd * LANES + ROPE_DIM].set(r)
        wuq = wuq.at[:, UQ_B + hd * LANES:UQ_B + hd * LANES + half].set(-r[:, half:])
        wuq = wuq.at[:, UQ_B + hd * LANES + half:UQ_B + hd * LANES + ROPE_DIM].set(r[:, :half])
        wuk = wuk.at[hd * NOPE_DIM:(hd + 1) * NOPE_DIM, hd * KV_RANK:(hd + 1) * KV_RANK].set(w_uk[:, hd, :].T)
    return wp.astype(jnp.bfloat16), wuq.astype(jnp.bfloat16), wuk.astype(jnp.bfloat16)


def _rope_tables(pos):
    half = ROPE_DIM // 2
    freqs = ROPE_THETA ** (-jnp.arange(half, dtype=jnp.float32) / half)
    ang = pos.astype(jnp.float32)[:, None] * freqs
    z = jnp.zeros((pos.shape[0], LANES - ROPE_DIM), jnp.float32)
    return (jnp.concatenate([jnp.cos(ang), jnp.cos(ang), z], 1), jnp.concatenate([jnp.sin(ang), jnp.sin(ang), z], 1))


def _idx_attn_kernel(topk, iq_ref, misc_ref, ik_ref, aq_ref, kv_ref, bnear_ref, bfar_ref, o_ref, keys_ref):
    f32, bf16, i32 = jnp.float32, jnp.bfloat16, jnp.int32
    qi = pl.program_id(1)
    t0 = qi * TQ
    n_c = (t0 + TQ + TK - 1) // TK
    n_keys = keys_ref.shape[1]
    t_pos = t0 + lax.broadcasted_iota(i32, (TQ, 1), 0)
    misc = misc_ref[0]
    nt = (((1,), (1,)), ((), ()))

    def score_chunk(c, carry):
        base = pl.multiple_of(c * TK, TK)
        kb = ik_ref[0, pl.ds(base, TK), :]
        sc = jnp.zeros((TQ, TK), f32)
        for hd in range(IDX_HEADS):
            s = lax.dot_general(iq_ref[0, :, hd * LANES:(hd + 1) * LANES], kb, nt, preferred_element_type=f32)
            sc = sc + jnp.maximum(s, 0.0) * misc[:, IDX_DIM + hd:IDX_DIM + hd + 1]
        s_pos = base + lax.broadcasted_iota(i32, (TQ, TK), 1)
        sc = jnp.where(s_pos <= t_pos, sc + 0.0, -jnp.inf)
        bits = lax.bitcast_convert_type(sc, i32)
        keys_ref[:, pl.ds(base, TK)] = jnp.where(bits < 0, bits ^ 0x7FFFFFFF, bits)
        return carry

    lax.fori_loop(0, n_c, score_chunk, 0)

    def fold(m):
        return m[:, 0:LANES] + m[:, LANES:2 * LANES] + m[:, 2 * LANES:3 * LANES] + m[:, 3 * LANES:4 * LANES]

    def count(pred):
        def body(c, acc):
            base = pl.multiple_of(c * TK, TK)
            k = keys_ref[:, pl.ds(base, TK)]
            idx = base + lax.broadcasted_iota(i32, (TQ, TK), 1)
            return acc + fold(jnp.where(pred(k, idx), 1.0, 0.0))
        acc = lax.fori_loop(0, n_c, body, jnp.zeros((TQ, LANES), f32))
        return jnp.sum(acc, axis=1, keepdims=True)

    kf = float(topk)
    ans0 = jnp.where(count(lambda k, idx: k >= 0) >= kf, 0, INT_MIN).astype(i32)

    def bit_step(i, ans):
        cand = ans | jnp.left_shift(jnp.int32(1), 30 - i)
        return jnp.where(count(lambda k, idx: k >= cand) >= kf, cand, ans)

    tau = lax.fori_loop(0, 31, bit_step, ans0)
    n_gt = count(lambda k, idx: k > tau)
    n_eq = count(lambda k, idx: k == tau)
    need = kf - n_gt

    idx_bits = n_keys.bit_length()

    def tie_search(_):
        def step(i, x):
            cand = x | jnp.left_shift(jnp.int32(1), idx_bits - 1 - i)
            return jnp.where(count(lambda k, idx: (k == tau) & (idx < cand)) < need, cand, x)
        return lax.fori_loop(0, idx_bits, step, jnp.zeros((TQ, 1), i32))

    any_excess = jnp.max(jnp.where(n_eq > need, 1.0, 0.0)) > 0.5
    j_cut = lax.cond(any_excess, tie_search, lambda _: jnp.full((TQ, 1), n_keys, i32), 0)

    q4 = jnp.concatenate([aq_ref[0, :, hd * LANES:(hd + 1) * LANES] for hd in range(A_HEADS)], axis=0)
    near0 = t0 - TQ

    def attend(carry, kv, k, idx, bias, limit_lo, limit_hi):
        m, l, acc = carry
        sel = ((k > tau) | ((k == tau) & (idx <= j_cut))) & (idx >= limit_lo) & (idx < limit_hi) & (idx <= t_pos)
        w = kv.shape[0]
        s = lax.dot_general(q4, kv[:, :LANES], nt, preferred_element_type=f32).reshape(A_HEADS, TQ, w)
        s = jnp.where(sel[None], s + bias, -jnp.inf)
        m_new = jnp.maximum(m, jnp.max(s, axis=-1, keepdims=True))
        m_safe = jnp.where(m_new == -jnp.inf, 0.0, m_new)
        p = jnp.exp(s - m_safe)
        corr = jnp.exp(m - m_safe)
        l = corr * l + jnp.sum(p, axis=-1, keepdims=True)
        pv = jnp.dot(p.reshape(A_HEADS * TQ, w).astype(bf16), kv[:, LANES:], preferred_element_type=f32)
        acc = corr * acc + pv.reshape(A_HEADS, TQ, LANES)
        return m_new, l, acc

    carry0 = (jnp.full((A_HEADS, TQ, 1), -jnp.inf, f32), jnp.zeros((A_HEADS, TQ, 1), f32), jnp.zeros((A_HEADS, TQ, LANES), f32))
    bias_far = bfar_ref[...]

    def far_chunk(c, carry):
        base = pl.multiple_of(c * TK, TK)
        idx = base + lax.broadcasted_iota(i32, (TQ, TK), 1)
        return attend(carry, kv_ref[0, pl.ds(base, TK), :], keys_ref[:, pl.ds(base, TK)], idx, bias_far[:, :, :1], 0, near0)

    carry = lax.fori_loop(0, (jnp.maximum(near0, 0) + TK - 1) // TK, far_chunk, carry0)
    prev = pl.multiple_of(jnp.maximum(near0, 0), TQ)
    diag = pl.multiple_of(t0, TQ)
    kv_near = jnp.concatenate([kv_ref[0, pl.ds(prev, TQ), :], kv_ref[0, pl.ds(diag, TQ), :]], axis=0)
    k_near = jnp.concatenate([keys_ref[:, pl.ds(prev, TQ)], keys_ref[:, pl.ds(diag, TQ)]], axis=1)
    idx_near = near0 + lax.broadcasted_iota(i32, (TQ, 2 * TQ), 1)
    m, l, acc = attend(carry, kv_near, k_near, idx_near, bnear_ref[...], 0, n_keys)
    out = acc / l
    lane = lax.broadcasted_iota(i32, (TQ, LANES), 1)
    blk_a = jnp.where(lane < HEAD_DIM, out[0], pltpu.roll(out[1], HEAD_DIM, 1))
    blk_b = jnp.where(lane < HEAD_DIM, pltpu.roll(out[2], HEAD_DIM, 1), out[3])
    o_ref[0] = jnp.concatenate([blk_a, blk_b], axis=1).astype(o_ref.dtype)


def idx_attn(iq, misc, ik, aq, kv, bias_near, bias_far, topk):
    b_, t_ = iq.shape[0], iq.shape[1]

    def qspec(w):
        return pl.BlockSpec((1, TQ, w), lambda b, i: (b, i, 0))

    def kspec(w):
        return pl.BlockSpec((1, t_, w), lambda b, i: (b, 0, 0))

    return pl.pallas_call(
        functools.partial(_idx_attn_kernel, topk),
        grid=(b_, t_ // TQ),
        in_specs=[qspec(IDX_HEADS * LANES), qspec(LANES), kspec(LANES), qspec(A_HEADS * LANES), kspec(2 * LANES),
                  _const_spec(bias_near.shape), _const_spec(bias_far.shape)],
        out_specs=qspec(A_WIDTH),
        out_shape=jax.ShapeDtypeStruct((b_, t_, A_WIDTH), jnp.bfloat16),
        scratch_shapes=[pltpu.VMEM((TQ, t_), jnp.int32)],
        compiler_params=_params(2),
        name="idx_attn",
    )(iq, misc, ik, aq, kv, bias_near, bias_far)


def _t5_bucket(dist):
    n = jnp.maximum(dist, 0)
    max_exact = REL_BUCKETS // 2
    n_f = jnp.maximum(n, 1).astype(jnp.float32)
    log_bucket = max_exact + (jnp.log(n_f / max_exact) / math.log(REL_MAX_DIST / max_exact) * (REL_BUCKETS - max_exact)).astype(jnp.int32)
    return jnp.where(n < max_exact, n, jnp.minimum(log_bucket, REL_BUCKETS - 1))


def _bias_tables(rel_bias):
    assert TQ >= REL_MAX_DIST
    dist = TQ + jnp.arange(TQ, dtype=jnp.int32)[:, None] - jnp.arange(2 * TQ, dtype=jnp.int32)[None, :]
    near = rel_bias[_t5_bucket(dist)].astype(jnp.float32).transpose(2, 0, 1)
    far = rel_bias[_t5_bucket(jnp.int32(2 * TQ))].astype(jnp.float32)
    return near, jnp.broadcast_to(far[:, None, None], (A_HEADS, 1, LANES))


def _mla_kernel(q_ref, k_ref, wuv_ref, o_ref):
    f32, bf16, i32 = jnp.float32, jnp.bfloat16, jnp.int32
    t0 = pl.program_id(1) * TQ
    w = 2 * LANES
    q4 = jnp.concatenate([q_ref[0, :, hd * w:(hd + 1) * w] for hd in range(B_HEADS)], axis=0)
    t_pos = t0 + lax.broadcasted_iota(i32, (TQ, 1), 0)

    def chunk(c, carry):
        m, l, acc = carry
        base = pl.multiple_of(c * TK, TK)
        kb = k_ref[0, pl.ds(base, TK), :]
        s = lax.dot_general(q4, kb, (((1,), (1,)), ((), ())), preferred_element_type=f32).reshape(B_HEADS, TQ, TK) * MLA_SCALE
        s_pos = base + lax.broadcasted_iota(i32, (TQ, TK), 1)
        s = jnp.where((s_pos <= t_pos)[None], s, -jnp.inf)
        m_new = jnp.maximum(m, jnp.max(s, axis=-1, keepdims=True))
        p = jnp.exp(s - m_new)
        corr = jnp.exp(m - m_new)
        l = corr * l + jnp.sum(p, axis=-1, keepdims=True)
        pv = jnp.dot(p.reshape(B_HEADS * TQ, TK).astype(bf16), kb[:, :KV_RANK], preferred_element_type=f32)
        return m_new, l, corr * acc + pv.reshape(B_HEADS, TQ, KV_RANK)

    carry0 = (jnp.full((B_HEADS, TQ, 1), -jnp.inf, f32), jnp.zeros((B_HEADS, TQ, 1), f32), jnp.zeros((B_HEADS, TQ, KV_RANK), f32))
    m, l, acc = lax.fori_loop(0, (t0 + TQ + TK - 1) // TK, chunk, carry0)
    o_lat = (acc / l).astype(bf16)
    out = jnp.zeros((TQ, B_WIDTH), f32)
    for hd in range(B_HEADS):
        out = out + jnp.dot(o_lat[hd], wuv_ref[hd], preferred_element_type=f32)
    o_ref[0] = out.astype(o_ref.dtype)


def mla_attn(qmla, kmla, wuv):
    b_, t_ = qmla.shape[0], qmla.shape[1]
    return pl.pallas_call(
        _mla_kernel,
        grid=(b_, t_ // TQ),
        in_specs=[pl.BlockSpec((1, TQ, qmla.shape[2]), lambda b, i: (b, i, 0)),
                  pl.BlockSpec((1, t_, 2 * LANES), lambda b, i: (b, 0, 0)),
                  _const_spec(wuv.shape)],
        out_specs=pl.BlockSpec((1, TQ, B_WIDTH), lambda b, i: (b, i, 0)),
        out_shape=jax.ShapeDtypeStruct((b_, t_, B_WIDTH), jnp.bfloat16),
        compiler_params=_params(2),
        name="mla_attn",
    )(qmla, kmla, wuv)


def _wuv_blocks(w_uv):
    out = jnp.zeros((B_HEADS, KV_RANK, B_WIDTH), jnp.float32)
    for hd in range(B_HEADS):
        out = out.at[hd, :, hd * V_DIM:(hd + 1) * V_DIM].set(w_uv[:, hd, :])
    return out.astype(jnp.bfloat16)


def _s5_kernel(n_levels, chunks_per_seq, u_ref, w_ref, wy_ref, apow_ref, y_ref, s_ref):
    f32, i32 = jnp.float32, jnp.int32
    n_rows = u_ref.shape[1]
    cw = S5_CHUNK * C_GROUP
    ye = jnp.dot(u_ref[0], w_ref[0], preferred_element_type=f32)
    e = ye[:, cw:]
    row = lax.broadcasted_iota(i32, (n_rows, 1), 0) % chunks_per_seq
    lane = lax.broadcasted_iota(i32, (1, 2 * C_STATE), 1)
    sign = jnp.where(lane < C_STATE, -1.0, 1.0)
    s = e
    for lv in range(n_levels):
        d = 1 << lv
        a = apow_ref[0, lv:lv + 1, :]
        a_re = jnp.concatenate([a[:, :C_STATE], a[:, :C_STATE]], axis=1)
        a_im = jnp.concatenate([a[:, C_STATE:], a[:, C_STATE:]], axis=1) * sign
        sh = jnp.where(row >= d, pltpu.roll(s, d, 0), 0.0)
        s = s + sh * a_re + pltpu.roll(sh, C_STATE, 1) * a_im
    s_ref[0] = s
    s_prev = jnp.where(row >= 1, pltpu.roll(s, 1, 0), 0.0)
    y_ref[0] = ye[:, :cw] + jnp.dot(s_prev.astype(jnp.bfloat16), wy_ref[0], preferred_element_type=f32)


def s5_prompt(u_g, w_cat, w_y, a_pow, chunks_per_seq):
    g_, n_rows, cw = u_g.shape
    n_levels = a_pow.shape[1]
    return pl.pallas_call(
        functools.partial(_s5_kernel, n_levels, chunks_per_seq),
        grid=(g_,),
        in_specs=[pl.BlockSpec((1, n_rows, cw), lambda g: (g, 0, 0)),
                  pl.BlockSpec((1, cw, cw + 2 * C_STATE), lambda g: (g, 0, 0)),
                  pl.BlockSpec((1, 2 * C_STATE, cw), lambda g: (g, 0, 0)),
                  pl.BlockSpec((1, n_levels, 2 * C_STATE), lambda g: (g, 0, 0))],
        out_specs=[pl.BlockSpec((1, n_rows, cw), lambda g: (g, 0, 0)),
                   pl.BlockSpec((1, n_rows, 2 * C_STATE), lambda g: (g, 0, 0))],
        out_shape=[jax.ShapeDtypeStruct((g_, n_rows, cw), jnp.float32),
                   jax.ShapeDtypeStruct((g_, n_rows, 2 * C_STATE), jnp.float32)],
        compiler_params=_params(),
        name="s5_prompt",
    )(u_g, w_cat, w_y, a_pow)


def _s5_weights(lam_re, lam_im, log_dt, b_re, b_im, c_re, c_im, d_skip, n_levels):
    f32 = jnp.float32
    L = S5_CHUNK
    lam = lax.complex(lam_re.astype(f32), lam_im.astype(f32))
    dt = jnp.exp(log_dt.astype(f32))[:, None]
    a_bar = jnp.exp(lam * dt)
    b_bar = ((a_bar - 1.0) / lam)[..., None] * lax.complex(b_re.astype(f32), b_im.astype(f32))
    c_mat = lax.complex(c_re.astype(f32), c_im.astype(f32))
    taus = jnp.arange(L + 1, dtype=f32)
    a_pw = jnp.exp((lam * dt)[None] * taus[:, None, None])
    kern = jnp.einsum('gcp,tgp,gpd->tgcd', c_mat, a_pw[:L], b_bar).real
    t_idx = jnp.arange(L)
    diff = t_idx[None, :] - t_idx[:, None]
    toe = kern[jnp.clip(diff, 0, L - 1)]
    toe = jnp.where((diff >= 0)[:, :, None, None, None], toe, 0.0)
    eye = (diff == 0)[:, :, None, None, None] * (jnp.eye(C_GROUP, dtype=f32)[None, None, None] * d_skip.astype(f32)[None, None, :, :, None])
    toe = (toe + eye).transpose(2, 0, 4, 1, 3).reshape(C_GROUPS, L * C_GROUP, L * C_GROUP)
    e_c = a_pw[L - 1 - t_idx][:, :, :, None] * b_bar[None]
    e_c = e_c.transpose(1, 0, 3, 2).reshape(C_GROUPS, L * C_GROUP, C_STATE)
    w_cat = jnp.concatenate([toe, e_c.real, e_c.imag], axis=2)
    y_c = c_mat[None] * a_pw[1:L + 1][:, :, None, :]
    y_c = y_c.transpose(1, 3, 0, 2).reshape(C_GROUPS, C_STATE, L * C_GROUP)
    w_y = jnp.concatenate([y_c.real, -y_c.imag], axis=1)
    lv = (L * (2.0 ** jnp.arange(n_levels, dtype=f32)))
    a_lv = jnp.exp((lam * dt)[:, None, :] * lv[None, :, None])
    a_pow = jnp.concatenate([a_lv.real, a_lv.imag], axis=2)
    return w_cat.astype(jnp.bfloat16), w_y.astype(jnp.bfloat16), a_pow


def _outproj_kernel(x_ref, gate_ref, oa_ref, ob_ref, ys_ref, wglu_ref, bglu_ref, woa_ref, wob_ref, woc_ref, lng_ref, lnb_ref, o_ref):
    f32, bf16 = jnp.float32, jnp.bfloat16
    y = ys_ref[...]
    z = 0.5 * y * (1.0 + jnp.tanh(math.sqrt(2.0 / math.pi) * (y + 0.044715 * (y * y * y))))
    yc = z * jax.nn.sigmoid(jnp.dot(z.astype(bf16), wglu_ref[...], preferred_element_type=f32) + bglu_ref[...])
    mix = jnp.dot(oa_ref[...], woa_ref[...], preferred_element_type=f32)
    mix = mix + jnp.dot(ob_ref[...], wob_ref[...], preferred_element_type=f32)
    mix = mix + jnp.dot(yc.astype(bf16), woc_ref[...], preferred_element_type=f32)
    o_ref[...] = _ln(ALPHA * x_ref[...] + (1.0 + gate_ref[0]) * mix, lng_ref[...], lnb_ref[...])


def outproj(x, gate, oa, ob, ys, wglu, bglu, w_out, ln_g, ln_b, rows_per_group):
    n = x.shape[0]
    rows = min(ROWS, n)
    bf16 = jnp.bfloat16

    def rspec(w):
        return pl.BlockSpec((rows, w), lambda i: (i, 0))

    woa, wob, woc = w_out[:A_WIDTH].astype(bf16), w_out[A_WIDTH:A_WIDTH + B_WIDTH].astype(bf16), w_out[A_WIDTH + B_WIDTH:].astype(bf16)
    vec = _const_spec((1, D_MODEL))
    return pl.pallas_call(
        _outproj_kernel,
        grid=(n // rows,),
        in_specs=[rspec(D_MODEL), _mod_spec(gate, rows, rows_per_group), rspec(A_WIDTH), rspec(B_WIDTH), rspec(C_WIDTH),
                  _const_spec((C_WIDTH, C_WIDTH)), _const_spec((1, C_WIDTH)),
                  _const_spec(woa.shape), _const_spec(wob.shape), _const_spec(woc.shape), vec, vec],
        out_specs=rspec(D_MODEL),
        out_shape=jax.ShapeDtypeStruct((n, D_MODEL), jnp.float32),
        compiler_params=_params(),
        name="outproj",
    )(x, gate, oa, ob, ys, wglu.astype(bf16), bglu.reshape(1, C_WIDTH), woa, wob, woc, ln_g.reshape(1, D_MODEL), ln_b.reshape(1, D_MODEL))


def _prompt_trunk(x, mod, p):
    b_, t_ = x.shape[0], x.shape[1]
    n = b_ * t_
    xf = x.reshape(n, D_MODEL)
    pos = jnp.arange(t_, dtype=jnp.int32)
    cos, sin = _rope_tables(pos)
    cos, sin = jnp.tile(cos, (b_, 1)), jnp.tile(sin, (b_, 1))
    bias_near, bias_far = _bias_tables(p['rel_bias'])
    topk = min(TOPK_MAX, t_ // 4)
    n_chunks = t_ // S5_CHUNK
    n_levels = max(1, (n_chunks - 1).bit_length())
    outs = []
    for l in range(DEPTH):
        m = mod[l].reshape(b_, 3, 3, 1, D_MODEL)

        def mo(j, k):
            return m[:, j, k]

        xf = ffn_block(xf, mo(0, 0), mo(0, 1), mo(0, 2), *p['ffn_w'][l][0], p['ln_g'][l, 0], p['ln_b'][l, 0], t_,
                       ln_in=(p['ln_in_g'], p['ln_in_b']) if l == 0 else None)
        wp, wuq, wuk = p['mix_w'][l]
        aq, ak, av, kv, iq, misc, ik, ckv, kpe, kmla, qmla, cu = mixproj(
            xf, mo(1, 0), mo(1, 1), wp, p['g_qnorm'][l], p['g_kvnorm'][l], wuq, wuk, cos, sin, t_)

        def seq(a):
            return a.reshape(b_, t_, a.shape[-1])

        oa = idx_attn(seq(iq), seq(misc), seq(ik), seq(aq), seq(kv), bias_near, bias_far, topk)
        ob = mla_attn(seq(qmla), seq(kmla), p['wuv'][l])
        w_cat, w_y, a_pow = _s5_weights(p['ssm_lam_re'][l], p['ssm_lam_im'][l], p['ssm_log_dt'][l], p['ssm_b_re'][l], p['ssm_b_im'][l],
                                        p['ssm_c_re'][l], p['ssm_c_im'][l], p['ssm_d'][l], n_levels)
        u_g = cu.astype(jnp.bfloat16).reshape(b_ * n_chunks, S5_CHUNK, C_GROUPS, C_GROUP).transpose(2, 0, 1, 3).reshape(
            C_GROUPS, b_ * n_chunks, S5_CHUNK * C_GROUP)
        y_g, s_g = s5_prompt(u_g, w_cat, w_y, a_pow, n_chunks)
        ys = y_g.reshape(C_GROUPS, b_ * n_chunks, S5_CHUNK, C_GROUP).transpose(1, 2, 0, 3).reshape(n, C_WIDTH)
        s_last = s_g.reshape(C_GROUPS, b_, n_chunks, 2 * C_STATE)[:, :, -1].transpose(1, 0, 2)
        xf = outproj(xf, mo(1, 2), oa.reshape(n, A_WIDTH), ob.reshape(n, B_WIDTH), ys, p['w_glu'][l], p['b_glu'][l], p['w_out'][l],
                     p['ln_g'][l, 1], p['ln_b'][l, 1], t_)
        xf = ffn_block(xf, mo(2, 0), mo(2, 1), mo(2, 2), *p['ffn_w'][l][1], p['ln_g'][l, 2], p['ln_b'][l, 2], t_)
        outs.append((ak.reshape(b_, t_, A_KV_HEADS, HEAD_DIM), av.reshape(b_, t_, A_KV_HEADS, HEAD_DIM),
                     seq(misc)[:, :, :IDX_DIM], seq(ckv), seq(kpe)[:, :, :ROPE_DIM], s_last[:, :, :C_STATE], s_last[:, :, C_STATE:]))
    stacked = [jnp.stack(s, axis=0) for s in zip(*outs)]
    return xf.reshape(b_, t_, D_MODEL), stacked


def _layer_norm(x, g, b):
    xf = x.astype(jnp.float32)
    mu = jnp.mean(xf, -1, keepdims=True)
    var = jnp.mean(jnp.square(xf - mu), -1, keepdims=True)
    return ((xf - mu) * lax.rsqrt(var + LN_EPS) * g + b).astype(x.dtype)


def _rms_norm(x, g):
    xf = x.astype(jnp.float32)
    return (xf * lax.rsqrt(jnp.mean(xf * xf, -1, keepdims=True) + 1e-6) * g).astype(x.dtype)


def _rope(x, pos):
    half = ROPE_DIM // 2
    freqs = ROPE_THETA ** (-jnp.arange(half, dtype=jnp.float32) / half)
    ang = pos.astype(jnp.float32)[:, None] * freqs
    cos = jnp.cos(ang)[:, None, :]
    sin = jnp.sin(ang)[:, None, :]
    x1 = x[..., :half].astype(jnp.float32)
    x2 = x[..., half:].astype(jnp.float32)
    return jnp.concatenate([x1 * cos - x2 * sin, x1 * sin + x2 * cos], -1).astype(x.dtype)


def _gather_rows(rows, sel):
    return jax.vmap(lambda r, i: r[i])(rows, sel)


def _gather_paged_all(pool, l, page_table):
    g = pool[l, page_table]
    return g.reshape(g.shape[0], g.shape[1] * g.shape[2], *g.shape[3:])


def _gather_paged_rows(pool, l, page_table, pos):
    db = page_table.shape[0]
    phys = jnp.take_along_axis(page_table, (pos // PAGE_SIZE).reshape(db, -1), axis=1).reshape(pos.shape)
    return pool[l, phys, pos % PAGE_SIZE]


def _project_groups(h, pos, w_in, g_qn, g_kvn, w_uq, w_uk):
    b_, t_ = h.shape[0], h.shape[1]
    p = jnp.einsum('btd,de->bte', h, w_in)
    split_at = [int(s) for s in np.cumsum(IN_SPLITS)[:-1]]
    a_q, a_k, a_v, i_q, i_w, i_k, b_cq, b_ckv, b_kpe, c_u = jnp.split(p, split_at, axis=-1)
    a_q = a_q.reshape(b_, t_, A_HEADS, HEAD_DIM)
    a_k = a_k.reshape(b_, t_, A_KV_HEADS, HEAD_DIM)
    a_v = a_v.reshape(b_, t_, A_KV_HEADS, HEAD_DIM)
    i_q = i_q.reshape(b_, t_, IDX_HEADS, IDX_DIM) * (IDX_DIM ** -0.5)
    i_w = i_w * (IDX_HEADS ** -0.5)
    cq = _rms_norm(b_cq, g_qn)
    q = jnp.einsum('btr,rhe->bthe', cq, w_uq)
    q_pe = _rope(q[..., NOPE_DIM:], pos)
    q_lat = jnp.einsum('bthn,rhn->bthr', q[..., :NOPE_DIM], w_uk)
    ckv = _rms_norm(b_ckv, g_kvn)
    kpe = _rope(b_kpe[:, :, None, :], pos)[:, :, 0]
    return a_q, a_k, a_v, i_q, i_w, i_k, q_lat, q_pe, ckv, kpe, c_u


def _indexer_topk(q_idx, w_idx, k_idx, q_pos, k_pos, topk):
    s = jax.nn.relu(jnp.einsum('bthd,bsd->bths', q_idx, k_idx).astype(jnp.float32))
    score = jnp.einsum('bths,bth->bts', s, w_idx.astype(jnp.float32))
    score = jnp.where(k_pos[None, None, :] <= q_pos[None, :, None], score, -jnp.inf)
    _, sel = lax.top_k(score, topk)
    return sel


def _sparse_attend(q, k_sel, v_sel, sel, q_pos, rel_bias):
    b_, t_, n_sel = sel.shape
    grp = A_HEADS // A_KV_HEADS
    qg = q.reshape(b_, t_, A_KV_HEADS, grp, HEAD_DIM)
    logits = jnp.einsum('btcgd,btncd->btcgn', qg, k_sel).astype(jnp.float32) * (HEAD_DIM ** -0.5)
    dist = q_pos[None, :, None] - sel
    bias = rel_bias[_t5_bucket(dist)].astype(jnp.float32)
    bias = bias.reshape(b_, t_, n_sel, A_KV_HEADS, grp).transpose(0, 1, 3, 4, 2)
    logits = jnp.where((dist >= 0)[:, :, None, None, :], logits + bias, -jnp.inf)
    p = jax.nn.softmax(logits, axis=-1).astype(v_sel.dtype)
    o = jnp.einsum('btcgn,btncd->btcgd', p, v_sel)
    return o.reshape(b_, t_, A_WIDTH)


def _mla_attend(q_lat, q_pe, ckv, kpe, q_pos, k_pos):
    logits = (jnp.einsum('bthr,bsr->bhts', q_lat, ckv) + jnp.einsum('bthe,bse->bhts', q_pe, kpe)).astype(jnp.float32) * MLA_SCALE
    logits = jnp.where(k_pos[None, None, None, :] <= q_pos[None, None, :, None], logits, -jnp.inf)
    p = jax.nn.softmax(logits, axis=-1).astype(ckv.dtype)
    return jnp.einsum('bhts,bsr->bthr', p, ckv)


def _s5_step(u, s0_re, s0_im, lam_re, lam_im, log_dt, b_re, b_im, c_re, c_im, d_skip, w_glu, b_glu):
    f32 = jnp.float32
    b_ = u.shape[0]
    lam = lax.complex(lam_re.astype(f32), lam_im.astype(f32))
    dt = jnp.exp(log_dt.astype(f32))[:, None]
    a_bar = jnp.exp(lam * dt)
    b_bar = ((a_bar - 1.0) / lam)[..., None] * lax.complex(b_re.astype(f32), b_im.astype(f32))
    c_mat = lax.complex(c_re.astype(f32), c_im.astype(f32))
    ug = u.astype(f32).reshape(b_, 1, C_GROUPS, C_GROUP)
    bu = jnp.einsum('gpc,btgc->btgp', b_bar, ug.astype(jnp.complex64))
    s0 = lax.complex(s0_re.astype(f32), s0_im.astype(f32))
    s = bu + a_bar * s0[:, None]
    y = jnp.einsum('gcp,btgp->btgc', c_mat, s).real + d_skip.astype(f32) * ug
    z = jax.nn.gelu(y.reshape(b_, 1, C_WIDTH))
    out = z * jax.nn.sigmoid(jnp.einsum('bte,ef->btf', z, w_glu) + b_glu)
    s_last = s[:, -1]
    return out.astype(u.dtype), jnp.real(s_last), jnp.imag(s_last)


def kernel(x_prompt, x_sample, c_prompt, c_sample, page_table, cache_attn_k, cache_attn_v, cache_idx_k, cache_mla_latent, cache_mla_rope, state_ssm_re, state_ssm_im, ln_in_g, ln_in_b, rel_bias, w_ada, b_ada, ln_g, ln_b, w_ffn_in, w_ffn_out, w_in, w_out, g_qnorm, g_kvnorm, w_uq, w_uk, w_uv, ssm_lam_re, ssm_lam_im, ssm_log_dt, ssm_b_re, ssm_b_im, ssm_c_re, ssm_c_im, ssm_d, w_glu, b_glu):
    n_p, n_s = c_prompt.shape[0], c_sample.shape[0]
    n_c = -(-(n_p + n_s) // 8) * 8
    c_all = jnp.concatenate([c_prompt, c_sample, jnp.zeros((n_c - n_p - n_s, D_MODEL), jnp.float32)], axis=0)
    mod_all = adaln(c_all, w_ada, b_ada)
    params = dict(
        ffn_w=[[_ffn_weights(w_ffn_in[l, j], w_ffn_out[l, j]) for j in range(2)] for l in range(DEPTH)],
        mix_w=[_mix_weights(w_in[l], w_uq[l], w_uk[l]) for l in range(DEPTH)],
        wuv=[_wuv_blocks(w_uv[l]) for l in range(DEPTH)],
        ln_in_g=ln_in_g, ln_in_b=ln_in_b, ln_g=ln_g, ln_b=ln_b, rel_bias=rel_bias, g_qnorm=g_qnorm, g_kvnorm=g_kvnorm,
        ssm_lam_re=ssm_lam_re, ssm_lam_im=ssm_lam_im, ssm_log_dt=ssm_log_dt, ssm_b_re=ssm_b_re, ssm_b_im=ssm_b_im,
        ssm_c_re=ssm_c_re, ssm_c_im=ssm_c_im, ssm_d=ssm_d, w_glu=w_glu, b_glu=b_glu, w_out=w_out)
    y_prompt, st_p = _prompt_trunk(x_prompt, mod_all[:, :n_p], params)

    def ssm_layer(u, s0_re, s0_im, l):
        return _s5_step(u, s0_re, s0_im, ssm_lam_re[l], ssm_lam_im[l], ssm_log_dt[l], ssm_b_re[l], ssm_b_im[l], ssm_c_re[l], ssm_c_im[l], ssm_d[l], w_glu[l], b_glu[l])

    def mla_out(o_lat, l):
        o = jnp.einsum('bthr,rhv->bthv', o_lat, w_uv[l])
        return o.reshape(o.shape[0], o.shape[1], B_WIDTH)

    def sample_mix(h, l):
        n_dec = h.shape[1]
        past = page_table.shape[1] * PAGE_SIZE
        pos = past + jnp.arange(n_dec, dtype=jnp.int32)
        k_pos = jnp.arange(past + n_dec, dtype=jnp.int32)
        a_q, a_k, a_v, i_q, i_w, i_k, q_lat, q_pe, ckv, kpe, u = _project_groups(h, pos, w_in[l], g_qnorm[l], g_kvnorm[l], w_uq[l], w_uk[l])
        topk = min(TOPK_MAX, (past + n_dec) // 4)
        i_k_all = jnp.concatenate([_gather_paged_all(cache_idx_k, l, page_table), i_k], axis=1)
        sel = _indexer_topk(i_q, i_w, i_k_all, pos, k_pos, topk)
        is_past = sel < past
        sel_past = jnp.minimum(sel, past - 1)
        sel_new = jnp.clip(sel - past, 0, n_dec - 1)
        k_sel = jnp.where(is_past[..., None, None], _gather_paged_rows(cache_attn_k, l, page_table, sel_past), _gather_rows(a_k, sel_new))
        v_sel = jnp.where(is_past[..., None, None], _gather_paged_rows(cache_attn_v, l, page_table, sel_past), _gather_rows(a_v, sel_new))
        oa = _sparse_attend(a_q, k_sel, v_sel, sel, pos, rel_bias)
        ckv_all = jnp.concatenate([_gather_paged_all(cache_mla_latent, l, page_table), ckv], axis=1)
        kpe_all = jnp.concatenate([_gather_paged_all(cache_mla_rope, l, page_table), kpe], axis=1)
        ob = _mla_attend(q_lat, q_pe, ckv_all, kpe_all, pos, k_pos)
        y_c, s_re, s_im = ssm_layer(u, state_ssm_re[l], state_ssm_im[l], l)
        mixed = jnp.concatenate([oa, mla_out(ob, l), y_c], axis=-1)
        return mixed, (a_k, a_v, i_k, ckv, kpe, s_re, s_im)

    mod_s = mod_all[:, n_p:n_p + n_s]
    xs = x_sample.reshape(n_s, D_MODEL)
    states = []
    for l in range(DEPTH):
        m = mod_s[l].reshape(n_s, 3, 3, D_MODEL)

        def mo(j, k):
            return m[:, j, k].reshape(1, n_s, D_MODEL)

        xs = ffn_block(xs, mo(0, 0), mo(0, 1), mo(0, 2), *params['ffn_w'][l][0], ln_g[l, 0], ln_b[l, 0], 1,
                       ln_in=(ln_in_g, ln_in_b) if l == 0 else None)
        xx = xs.reshape(n_s, 1, D_MODEL)
        mixed, st = sample_mix(xx * (1.0 + m[:, 1, 1][:, None]) + m[:, 1, 0][:, None], l)
        y = jnp.einsum('bte,ed->btd', mixed, w_out[l])
        xx = _layer_norm(ALPHA * xx + (1.0 + m[:, 1, 2][:, None]) * y, ln_g[l, 1], ln_b[l, 1])
        xs = ffn_block(xx.reshape(n_s, D_MODEL), mo(2, 0), mo(2, 1), mo(2, 2), *params['ffn_w'][l][1], ln_g[l, 2], ln_b[l, 2], 1)
        states.append(st)
    st_s = [jnp.stack(s, axis=0) for s in zip(*states)]
    y_sample = xs.reshape(n_s, 1, D_MODEL)
    return (y_prompt, y_sample, st_p[0], st_s[0], st_p[1], st_s[1], st_p[2], st_s[2], st_p[3], st_s[3], st_p[4], st_s[4], st_p[5], st_s[5], st_p[6], st_s[6])
```

```python
import functools
import math

import jax
import jax.numpy as jnp
import numpy as np
from jax import lax
from jax.experimental import pallas as pl
from jax.experimental.pallas import tpu as pltpu

D_MODEL = 1024
BATCH = 2
SEQ = 8192
DEPTH = 2
DEC_BATCH = 128
DEC_SEQ = 1
PAST_LEN = 16384
PAGE_SIZE = 128
HEAD_DIM = 64
A_HEADS = 4
A_KV_HEADS = 2
IDX_HEADS = 4
IDX_DIM = 32
TOPK_MAX = 256
REL_BUCKETS = 32
REL_MAX_DIST = 128
B_HEADS = 4
Q_RANK = 256
KV_RANK = 128
NOPE_DIM = 64
ROPE_DIM = 32
V_DIM = 64
ROPE_THETA = 10000.0
A_WIDTH = A_HEADS * HEAD_DIM
B_WIDTH = B_HEADS * V_DIM
C_WIDTH = D_MODEL - A_WIDTH - B_WIDTH
C_GROUP = 16
C_GROUPS = C_WIDTH // C_GROUP
C_STATE = 64
D_FF = 2816
Q_BLOCK = 128
LN_EPS = 1e-5
ALPHA = (2.0 * DEPTH) ** 0.25
MLA_SCALE = (NOPE_DIM + ROPE_DIM) ** -0.5
IN_SPLITS = (A_HEADS * HEAD_DIM, A_KV_HEADS * HEAD_DIM, A_KV_HEADS * HEAD_DIM, IDX_HEADS * IDX_DIM, IDX_HEADS, IDX_DIM, Q_RANK, KV_RANK, ROPE_DIM, C_WIDTH)
D_IN = sum(IN_SPLITS)

VMEM_LIMIT_V7X = 56 * 1024 * 1024
LANES = 128
FF_CHUNK = 256
N_FF_CHUNKS = D_FF // FF_CHUNK
ROWS = 512
TQ = 128
TK = 512
S5_CHUNK = 16
INT_MIN = -2 ** 31

COL_AQ = 0
COL_AK = COL_AQ + A_HEADS * LANES
COL_AV = COL_AK + LANES
COL_IQ = COL_AV + LANES
COL_CQ = COL_IQ + IDX_HEADS * LANES
COL_CKV = COL_CQ + Q_RANK
COL_CU = COL_CKV + KV_RANK
COL_KA = COL_CU + C_WIDTH
COL_KB = COL_KA + LANES
COL_MISC = COL_KB + LANES
N_PROJ = COL_MISC + LANES
UQ_NOPE = 0
UQ_A = B_HEADS * NOPE_DIM
UQ_B = UQ_A + B_HEADS * LANES
N_UQ = UQ_B + B_HEADS * LANES


def _params(n_axes=1):
    return pltpu.CompilerParams(dimension_semantics=("arbitrary",) * n_axes, vmem_limit_bytes=VMEM_LIMIT_V7X)


def _ln(v, g, b):
    mu = jnp.mean(v, -1, keepdims=True)
    d = v - mu
    var = jnp.mean(d * d, -1, keepdims=True)
    return d * lax.rsqrt(var + LN_EPS) * g + b


def _mod_spec(mod, rows, rows_per_group):
    if mod.shape[1] == 1:
        return pl.BlockSpec((1, 1, D_MODEL), lambda i: (i // (rows_per_group // rows), 0, 0))
    return pl.BlockSpec((1, rows, D_MODEL), lambda i: (0, i, 0))


def _const_spec(shape):
    nd = len(shape)
    return pl.BlockSpec(shape, lambda *_: (0,) * nd, pipeline_mode=pl.Buffered(1))


ADA_COLS = 1152


def _adaln_kernel(c_ref, w_ref, b_ref, o_ref):
    c = c_ref[...]
    o_ref[0] = jnp.dot(c * jax.nn.sigmoid(c), w_ref[0], preferred_element_type=jnp.float32) + b_ref[0]


def adaln(c, w_ada, b_ada):
    n = c.shape[0]
    n_out = w_ada.shape[2]
    return pl.pallas_call(
        _adaln_kernel,
        grid=(DEPTH, n_out // ADA_COLS),
        in_specs=[pl.BlockSpec((n, D_MODEL), lambda l, j: (0, 0)),
                  pl.BlockSpec((1, D_MODEL, ADA_COLS), lambda l, j: (l, 0, j)),
                  pl.BlockSpec((1, 1, ADA_COLS), lambda l, j: (l, 0, j))],
        out_specs=pl.BlockSpec((1, n, ADA_COLS), lambda l, j: (l, 0, j)),
        out_shape=jax.ShapeDtypeStruct((DEPTH, n, n_out), jnp.float32),
        compiler_params=_params(2),
        name="adaln",
    )(c, w_ada, b_ada.reshape(DEPTH, 1, n_out))


def _ffn_kernel(pre_ln, x_ref, shift_ref, scale_ref, gate_ref, wg_ref, wu_ref, wd_ref, lng_ref, lnb_ref, *rest):
    if pre_ln:
        ing_ref, inb_ref, o_ref, acc_ref = rest
    else:
        o_ref, acc_ref = rest
    x = x_ref[...]
    if pre_ln:
        x = _ln(x, ing_ref[...], inb_ref[...])
    h = (x * (1.0 + scale_ref[0]) + shift_ref[0]).astype(jnp.bfloat16)
    acc_ref[...] = jnp.zeros_like(acc_ref)

    def chunk(c, carry):
        g = jnp.dot(h, wg_ref[c], preferred_element_type=jnp.float32)
        u = jnp.dot(h, wu_ref[c], preferred_element_type=jnp.float32)
        act = (g * jax.nn.sigmoid(g) * u).astype(jnp.bfloat16)
        acc_ref[...] += jnp.dot(act, wd_ref[c], preferred_element_type=jnp.float32)
        return carry

    lax.fori_loop(0, N_FF_CHUNKS, chunk, 0)
    y = ALPHA * x + 0.5 * (1.0 + gate_ref[0]) * acc_ref[...]
    o_ref[...] = _ln(y, lng_ref[...], lnb_ref[...])


def ffn_block(x, shift, scale, gate, wg, wu, wd, ln_g, ln_b, rows_per_group, ln_in=None):
    n = x.shape[0]
    rows = min(ROWS, n)
    pre_ln = ln_in is not None
    row_spec = pl.BlockSpec((rows, D_MODEL), lambda i: (i, 0))
    vec_spec = _const_spec((1, D_MODEL))
    in_specs = [row_spec] + [_mod_spec(m, rows, rows_per_group) for m in (shift, scale, gate)] + [
        _const_spec(wg.shape), _const_spec(wu.shape), _const_spec(wd.shape), vec_spec, vec_spec]
    args = [x, shift, scale, gate, wg, wu, wd, ln_g.reshape(1, D_MODEL), ln_b.reshape(1, D_MODEL)]
    if pre_ln:
        in_specs += [vec_spec, vec_spec]
        args += [ln_in[0].reshape(1, D_MODEL), ln_in[1].reshape(1, D_MODEL)]
    return pl.pallas_call(
        functools.partial(_ffn_kernel, pre_ln),
        grid=(n // rows,),
        in_specs=in_specs,
        out_specs=row_spec,
        out_shape=jax.ShapeDtypeStruct((n, D_MODEL), jnp.float32),
        scratch_shapes=[pltpu.VMEM((rows, D_MODEL), jnp.float32)],
        compiler_params=_params(),
        name="ffn_block",
    )(*args)


def _ffn_weights(w_in, w_out):
    wg = w_in[:, :D_FF].reshape(D_MODEL, N_FF_CHUNKS, FF_CHUNK).transpose(1, 0, 2).astype(jnp.bfloat16)
    wu = w_in[:, D_FF:].reshape(D_MODEL, N_FF_CHUNKS, FF_CHUNK).transpose(1, 0, 2).astype(jnp.bfloat16)
    wd = w_out.reshape(N_FF_CHUNKS, FF_CHUNK, D_MODEL).astype(jnp.bfloat16)
    return wg, wu, wd


def _mixproj_kernel(x_ref, shift_ref, scale_ref, wp_ref, gq_ref, gkv_ref, wuq_ref, wuk_ref, cos_ref, sin_ref,
                    aq_ref, ak_ref, av_ref, kv_ref, iq_ref, misc_ref, ik_ref, ckv_ref, kpe_ref, kmla_ref, qmla_ref, cu_ref):
    bf16 = jnp.bfloat16
    h = (x_ref[...] * (1.0 + scale_ref[0]) + shift_ref[0]).astype(bf16)
    p = jnp.dot(h, wp_ref[...], preferred_element_type=jnp.float32)
    aq_ref[...] = (p[:, COL_AQ:COL_AK] * (HEAD_DIM ** -0.5)).astype(bf16)
    ak = p[:, COL_AK:COL_AV]
    av = p[:, COL_AV:COL_IQ]
    ak_ref[...] = ak
    av_ref[...] = av
    kv_ref[...] = p[:, COL_AK:COL_IQ].astype(bf16)
    iq_ref[...] = (p[:, COL_IQ:COL_CQ] * (IDX_DIM ** -0.5)).astype(bf16)
    lane = lax.broadcasted_iota(jnp.int32, (1, LANES), 1)
    is_w = (lane >= IDX_DIM) & (lane < IDX_DIM + IDX_HEADS)
    misc = p[:, COL_MISC:N_PROJ] * jnp.where(is_w, IDX_HEADS ** -0.5, 1.0)
    misc_ref[...] = misc
    ik_ref[...] = misc.astype(bf16)
    cos = cos_ref[...]
    sin = sin_ref[...]
    kpe = p[:, COL_KA:COL_KB] * cos + p[:, COL_KB:COL_MISC] * sin
    kpe_ref[...] = kpe
    ckv_in = p[:, COL_CKV:COL_CU]
    ckv = ckv_in * lax.rsqrt(jnp.mean(ckv_in * ckv_in, -1, keepdims=True) + 1e-6) * gkv_ref[...]
    ckv_ref[...] = ckv
    kmla_ref[...] = jnp.concatenate([ckv, kpe], axis=1).astype(bf16)
    cq_in = p[:, COL_CQ:COL_CKV]
    cq = (cq_in * lax.rsqrt(jnp.mean(cq_in * cq_in, -1, keepdims=True) + 1e-6) * gq_ref[...]).astype(bf16)
    q = jnp.dot(cq, wuq_ref[...], preferred_element_type=jnp.float32)
    q_lat = jnp.dot(q[:, UQ_NOPE:UQ_A].astype(bf16), wuk_ref[...], preferred_element_type=jnp.float32)
    pieces = []
    for hd in range(B_HEADS):
        q_pe = q[:, UQ_A + hd * LANES:UQ_A + (hd + 1) * LANES] * cos + q[:, UQ_B + hd * LANES:UQ_B + (hd + 1) * LANES] * sin
        pieces += [q_lat[:, hd * KV_RANK:(hd + 1) * KV_RANK], q_pe]
    qmla_ref[...] = jnp.concatenate(pieces, axis=1).astype(bf16)
    cu_ref[...] = p[:, COL_CU:COL_KA]


def mixproj(x, shift, scale, wp, gq, gkv, wuq, wuk, cos, sin, rows_per_group):
    n = x.shape[0]
    rows = min(ROWS, n)
    f32, bf16 = jnp.float32, jnp.bfloat16

    def rspec(w):
        return pl.BlockSpec((rows, w), lambda i: (i, 0))

    out_w = [(A_HEADS * LANES, bf16), (LANES, f32), (LANES, f32), (2 * LANES, bf16), (IDX_HEADS * LANES, bf16),
             (LANES, f32), (LANES, bf16), (KV_RANK, f32), (LANES, f32), (2 * LANES, bf16),
             (B_HEADS * 2 * LANES, bf16), (C_WIDTH, f32)]
    return pl.pallas_call(
        _mixproj_kernel,
        grid=(n // rows,),
        in_specs=[rspec(D_MODEL), _mod_spec(shift, rows, rows_per_group), _mod_spec(scale, rows, rows_per_group),
                  _const_spec(wp.shape), _const_spec((1, Q_RANK)), _const_spec((1, KV_RANK)),
                  _const_spec(wuq.shape), _const_spec(wuk.shape), rspec(LANES), rspec(LANES)],
        out_specs=[rspec(w) for w, _ in out_w],
        out_shape=[jax.ShapeDtypeStruct((n, w), dt) for w, dt in out_w],
        compiler_params=_params(),
        name="mixproj",
    )(x, shift, scale, wp, gq.reshape(1, Q_RANK), gkv.reshape(1, KV_RANK), wuq, wuk, cos, sin)


def _mix_weights(w_in, w_uq, w_uk):
    f32 = jnp.float32
    off = np.concatenate([[0], np.cumsum(IN_SPLITS)])
    a_q, a_k, a_v, i_q, i_w, i_k, b_cq, b_ckv, b_kpe, c_u = [w_in[:, off[i]:off[i + 1]] for i in range(10)]
    wp = jnp.zeros((D_MODEL, N_PROJ), f32)
    grp = A_HEADS // A_KV_HEADS
    for hd in range(A_HEADS):
        lo = COL_AQ + hd * LANES + HEAD_DIM * (hd // grp)
        wp = wp.at[:, lo:lo + HEAD_DIM].set(a_q[:, hd * HEAD_DIM:(hd + 1) * HEAD_DIM])
    wp = wp.at[:, COL_AK:COL_AK + LANES].set(a_k)
    wp = wp.at[:, COL_AV:COL_AV + LANES].set(a_v)
    for hd in range(IDX_HEADS):
        lo = COL_IQ + hd * LANES
        wp = wp.at[:, lo:lo + IDX_DIM].set(i_q[:, hd * IDX_DIM:(hd + 1) * IDX_DIM])
    wp = wp.at[:, COL_CQ:COL_CQ + Q_RANK].set(b_cq)
    wp = wp.at[:, COL_CKV:COL_CKV + KV_RANK].set(b_ckv)
    wp = wp.at[:, COL_CU:COL_CU + C_WIDTH].set(c_u)
    half = ROPE_DIM // 2
    wp = wp.at[:, COL_KA:COL_KA + ROPE_DIM].set(b_kpe)
    wp = wp.at[:, COL_KB:COL_KB + half].set(-b_kpe[:, half:])
    wp = wp.at[:, COL_KB + half:COL_KB + ROPE_DIM].set(b_kpe[:, :half])
    wp = wp.at[:, COL_MISC:COL_MISC + IDX_DIM].set(i_k)
    wp = wp.at[:, COL_MISC + IDX_DIM:COL_MISC + IDX_DIM + IDX_HEADS].set(i_w)
    wuq = jnp.zeros((Q_RANK, N_UQ), f32)
    wuk = jnp.zeros((B_HEADS * NOPE_DIM, B_HEADS * KV_RANK), f32)
    for hd in range(B_HEADS):
        wuq = wuq.at[:, UQ_NOPE + hd * NOPE_DIM:UQ_NOPE + (hd + 1) * NOPE_DIM].set(w_uq[:, hd, :NOPE_DIM])
        r = w_uq[:, hd, NOPE_DIM:]
        wuq = wuq.at[:, UQ_A + hd * LANES:UQ_A + hd * LANES + ROPE_DIM].set(r)
        wuq = wuq.at[:, UQ_B + hd * LANES:UQ_B + hd * LANES + half].set(-r[:, half:])
        wuq = wuq.at[:, UQ_B + hd * LANES + half:UQ_B + hd * LANES + ROPE_DIM].set(r[:, :half])
        wuk = wuk.at[hd * NOPE_DIM:(hd + 1) * NOPE_DIM, hd * KV_RANK:(hd + 1) * KV_RANK].set(w_uk[:, hd, :].T)
    return wp.astype(jnp.bfloat16), wuq.astype(jnp.bfloat16), wuk.astype(jnp.bfloat16)


def _rope_tables(pos):
    half = ROPE_DIM // 2
    freqs = ROPE_THETA ** (-jnp.arange(half, dtype=jnp.float32) / half)
    ang = pos.astype(jnp.float32)[:, None] * freqs
    z = jnp.zeros((pos.shape[0], LANES - ROPE_DIM), jnp.float32)
    return (jnp.concatenate([jnp.cos(ang), jnp.cos(ang), z], 1), jnp.concatenate([jnp.sin(ang), jnp.sin(ang), z], 1))


def _idx_attn_kernel(topk, iq_ref, misc_ref, ik_ref, aq_ref, kv_ref, bnear_ref, bfar_ref, o_ref, keys_ref):
    f32, bf16, i32 = jnp.float32, jnp.bfloat16, jnp.int32
    qi = pl.program_id(1)
    t0 = qi * TQ
    n_c = (t0 + TQ + TK - 1) // TK
    n_keys = keys_ref.shape[1]
    t_pos = t0 + lax.broadcasted_iota(i32, (TQ, 1), 0)
    misc = misc_ref[0]
    nt = (((1,), (1,)), ((), ()))

    def score_chunk(c, carry):
        base = pl.multiple_of(c * TK, TK)
        kb = ik_ref[0, pl.ds(base, TK), :]
        sc = jnp.zeros((TQ, TK), f32)
        for hd in range(IDX_HEADS):
            s = lax.dot_general(iq_ref[0, :, hd * LANES:(hd + 1) * LANES], kb, nt, preferred_element_type=f32)
            sc = sc + jnp.maximum(s, 0.0) * misc[:, IDX_DIM + hd:IDX_DIM + hd + 1]
        s_pos = base + lax.broadcasted_iota(i32, (TQ, TK), 1)
        sc = jnp.where(s_pos <= t_pos, sc + 0.0, -jnp.inf)
        bits = lax.bitcast_convert_type(sc, i32)
        keys_ref[:, pl.ds(base, TK)] = jnp.where(bits < 0, bits ^ 0x7FFFFFFF, bits)
        return carry

    lax.fori_loop(0, n_c, score_chunk, 0)

    def fold(m):
        return m[:, 0:LANES] + m[:, LANES:2 * LANES] + m[:, 2 * LANES:3 * LANES] + m[:, 3 * LANES:4 * LANES]

    def count(pred):
        def body(c, acc):
            base = pl.multiple_of(c * TK, TK)
            k = keys_ref[:, pl.ds(base, TK)]
            idx = base + lax.broadcasted_iota(i32, (TQ, TK), 1)
            return acc + fold(jnp.where(pred(k, idx), 1.0, 0.0))
        acc = lax.fori_loop(0, n_c, body, jnp.zeros((TQ, LANES), f32))
        return jnp.sum(acc, axis=1, keepdims=True)

    kf = float(topk)
    ans0 = jnp.where(count(lambda k, idx: k >= 0) >= kf, 0, INT_MIN).astype(i32)

    def bit_step(i, ans):
        cand = ans | jnp.left_shift(jnp.int32(1), 30 - i)
        return jnp.where(count(lambda k, idx: k >= cand) >= kf, cand, ans)

    tau = lax.fori_loop(0, 31, bit_step, ans0)
    n_gt = count(lambda k, idx: k > tau)
    n_eq = count(lambda k, idx: k == tau)
    need = kf - n_gt

    idx_bits = n_keys.bit_length()

    def tie_search(_):
        def step(i, x):
            cand = x | jnp.left_shift(jnp.int32(1), idx_bits - 1 - i)
            return jnp.where(count(lambda k, idx: (k == tau) & (idx < cand)) < need, cand, x)
        return lax.fori_loop(0, idx_bits, step, jnp.zeros((TQ, 1), i32))

    any_excess = jnp.max(jnp.where(n_eq > need, 1.0, 0.0)) > 0.5
    j_cut = lax.cond(any_excess, tie_search, lambda _: jnp.full((TQ, 1), n_keys, i32), 0)

    q4 = jnp.concatenate([aq_ref[0, :, hd * LANES:(hd + 1) * LANES] for hd in range(A_HEADS)], axis=0)
    near0 = t0 - TQ

    def attend(carry, kv, k, idx, bias, limit_lo, limit_hi):
        m, l, acc = carry
        sel = ((k > tau) | ((k == tau) & (idx <= j_cut))) & (idx >= limit_lo) & (idx < limit_hi) & (idx <= t_pos)
        w = kv.shape[0]
        s = lax.dot_general(q4, kv[:, :LANES], nt, preferred_element_type=f32).reshape(A_HEADS, TQ, w)
        s = jnp.where(sel[None], s + bias, -jnp.inf)
        m_new = jnp.maximum(m, jnp.max(s, axis=-1, keepdims=True))
        m_safe = jnp.where(m_new == -jnp.inf, 0.0, m_new)
        p = jnp.exp(s - m_safe)
        corr = jnp.exp(m - m_safe)
        l = corr * l + jnp.sum(p, axis=-1, keepdims=True)
        pv = jnp.dot(p.reshape(A_HEADS * TQ, w).astype(bf16), kv[:, LANES:], preferred_element_type=f32)
        acc = corr * acc + pv.reshape(A_HEADS, TQ, LANES)
        return m_new, l, acc

    carry0 = (jnp.full((A_HEADS, TQ, 1), -jnp.inf, f32), jnp.zeros((A_HEADS, TQ, 1), f32), jnp.zeros((A_HEADS, TQ, LANES), f32))
    bias_far = bfar_ref[...]

    def far_chunk(c, carry):
        base = pl.multiple_of(c * TK, TK)
        idx = base + lax.broadcasted_iota(i32, (TQ, TK), 1)
        return attend(carry, kv_ref[0, pl.ds(base, TK), :], keys_ref[:, pl.ds(base, TK)], idx, bias_far[:, :, :1], 0, near0)

    carry = lax.fori_loop(0, (jnp.maximum(near0, 0) + TK - 1) // TK, far_chunk, carry0)
    prev = pl.multiple_of(jnp.maximum(near0, 0), TQ)
    diag = pl.multiple_of(t0, TQ)
    kv_near = jnp.concatenate([kv_ref[0, pl.ds(prev, TQ), :], kv_ref[0, pl.ds(diag, TQ), :]], axis=0)
    k_near = jnp.concatenate([keys_ref[:, pl.ds(prev, TQ)], keys_ref[:, pl.ds(diag, TQ)]], axis=1)
    idx_near = near0 + lax.broadcasted_iota(i32, (TQ, 2 * TQ), 1)
    m, l, acc = attend(carry, kv_near, k_near, idx_near, bnear_ref[...], 0, n_keys)
    out = acc / l
    lane = lax.broadcasted_iota(i32, (TQ, LANES), 1)
    blk_a = jnp.where(lane < HEAD_DIM, out[0], pltpu.roll(out[1], HEAD_DIM, 1))
    blk_b = jnp.where(lane < HEAD_DIM, pltpu.roll(out[2], HEAD_DIM, 1), out[3])
    o_ref[0] = jnp.concatenate([blk_a, blk_b], axis=1).astype(o_ref.dtype)


def idx_attn(iq, misc, ik, aq, kv, bias_near, bias_far, topk):
    b_, t_ = iq.shape[0], iq.shape[1]

    def qspec(w):
        return pl.BlockSpec((1, TQ, w), lambda b, i: (b, i, 0))

    def kspec(w):
        return pl.BlockSpec((1, t_, w), lambda b, i: (b, 0, 0))

    return pl.pallas_call(
        functools.partial(_idx_attn_kernel, topk),
        grid=(b_, t_ // TQ),
        in_specs=[qspec(IDX_HEADS * LANES), qspec(LANES), kspec(LANES), qspec(A_HEADS * LANES), kspec(2 * LANES),
                  _const_spec(bias_near.shape), _const_spec(bias_far.shape)],
        out_specs=qspec(A_WIDTH),
        out_shape=jax.ShapeDtypeStruct((b_, t_, A_WIDTH), jnp.bfloat16),
        scratch_shapes=[pltpu.VMEM((TQ, t_), jnp.int32)],
        compiler_params=_params(2),
        name="idx_attn",
    )(iq, misc, ik, aq, kv, bias_near, bias_far)


def _t5_bucket(dist):
    n = jnp.maximum(dist, 0)
    max_exact = REL_BUCKETS // 2
    n_f = jnp.maximum(n, 1).astype(jnp.float32)
    log_bucket = max_exact + (jnp.log(n_f / max_exact) / math.log(REL_MAX_DIST / max_exact) * (REL_BUCKETS - max_exact)).astype(jnp.int32)
    return jnp.where(n < max_exact, n, jnp.minimum(log_bucket, REL_BUCKETS - 1))


def _bias_tables(rel_bias):
    assert TQ >= REL_MAX_DIST
    dist = TQ + jnp.arange(TQ, dtype=jnp.int32)[:, None] - jnp.arange(2 * TQ, dtype=jnp.int32)[None, :]
    near = rel_bias[_t5_bucket(dist)].astype(jnp.float32).transpose(2, 0, 1)
    far = rel_bias[_t5_bucket(jnp.int32(2 * TQ))].astype(jnp.float32)
    return near, jnp.broadcast_to(far[:, None, None], (A_HEADS, 1, LANES))


def _mla_kernel(q_ref, k_ref, wuv_ref, o_ref):
    f32, bf16, i32 = jnp.float32, jnp.bfloat16, jnp.int32
    t0 = pl.program_id(1) * TQ
    w = 2 * LANES
    q4 = jnp.concatenate([q_ref[0, :, hd * w:(hd + 1) * w] for hd in range(B_HEADS)], axis=0)
    t_pos = t0 + lax.broadcasted_iota(i32, (TQ, 1), 0)

    def chunk(c, carry):
        m, l, acc = carry
        base = pl.multiple_of(c * TK, TK)
        kb = k_ref[0, pl.ds(base, TK), :]
        s = lax.dot_general(q4, kb, (((1,), (1,)), ((), ())), preferred_element_type=f32).reshape(B_HEADS, TQ, TK) * MLA_SCALE
        s_pos = base + lax.broadcasted_iota(i32, (TQ, TK), 1)
        s = jnp.where((s_pos <= t_pos)[None], s, -jnp.inf)
        m_new = jnp.maximum(m, jnp.max(s, axis=-1, keepdims=True))
        p = jnp.exp(s - m_new)
        corr = jnp.exp(m - m_new)
        l = corr * l + jnp.sum(p, axis=-1, keepdims=True)
        pv = jnp.dot(p.reshape(B_HEADS * TQ, TK).astype(bf16), kb[:, :KV_RANK], preferred_element_type=f32)
        return m_new, l, corr * acc + pv.reshape(B_HEADS, TQ, KV_RANK)

    carry0 = (jnp.full((B_HEADS, TQ, 1), -jnp.inf, f32), jnp.zeros((B_HEADS, TQ, 1), f32), jnp.zeros((B_HEADS, TQ, KV_RANK), f32))
    m, l, acc = lax.fori_loop(0, (t0 + TQ + TK - 1) // TK, chunk, carry0)
    o_lat = (acc / l).astype(bf16)
    out = jnp.zeros((TQ, B_WIDTH), f32)
    for hd in range(B_HEADS):
        out = out + jnp.dot(o_lat[hd], wuv_ref[hd], preferred_element_type=f32)
    o_ref[0] = out.astype(o_ref.dtype)


def mla_attn(qmla, kmla, wuv):
    b_, t_ = qmla.shape[0], qmla.shape[1]
    return pl.pallas_call(
        _mla_kernel,
        grid=(b_, t_ // TQ),
        in_specs=[pl.BlockSpec((1, TQ, qmla.shape[2]), lambda b, i: (b, i, 0)),
                  pl.BlockSpec((1, t_, 2 * LANES), lambda b, i: (b, 0, 0)),
                  _const_spec(wuv.shape)],
        out_specs=pl.BlockSpec((1, TQ, B_WIDTH), lambda b, i: (b, i, 0)),
        out_shape=jax.ShapeDtypeStruct((b_, t_, B_WIDTH), jnp.bfloat16),
        compiler_params=_params(2),
        name="mla_attn",
    )(qmla, kmla, wuv)


def _wuv_blocks(w_uv):
    out = jnp.zeros((B_HEADS, KV_RANK, B_WIDTH), jnp.float32)
    for hd in range(B_HEADS):
        out = out.at[hd, :, hd * V_DIM:(hd + 1) * V_DIM].set(w_uv[:, hd, :])
    return out.astype(jnp.bfloat16)


def _s5_kernel(n_levels, chunks_per_seq, u_ref, w_ref, wy_ref, apow_ref, y_ref, s_ref):
    f32, i32 = jnp.float32, jnp.int32
    n_rows = u_ref.shape[1]
    cw = S5_CHUNK * C_GROUP
    ye = jnp.dot(u_ref[0], w_ref[0], preferred_element_type=f32)
    e = ye[:, cw:]
    row = lax.broadcasted_iota(i32, (n_rows, 1), 0) % chunks_per_seq
    lane = lax.broadcasted_iota(i32, (1, 2 * C_STATE), 1)
    sign = jnp.where(lane < C_STATE, -1.0, 1.0)
    s = e
    for lv in range(n_levels):
        d = 1 << lv
        a = apow_ref[0, lv:lv + 1, :]
        a_re = jnp.concatenate([a[:, :C_STATE], a[:, :C_STATE]], axis=1)
        a_im = jnp.concatenate([a[:, C_STATE:], a[:, C_STATE:]], axis=1) * sign
        sh = jnp.where(row >= d, pltpu.roll(s, d, 0), 0.0)
        s = s + sh * a_re + pltpu.roll(sh, C_STATE, 1) * a_im
    s_ref[0] = s
    s_prev = jnp.where(row >= 1, pltpu.roll(s, 1, 0), 0.0)
    y_ref[0] = ye[:, :cw] + jnp.dot(s_prev.astype(jnp.bfloat16), wy_ref[0], preferred_element_type=f32)


def s5_prompt(u_g, w_cat, w_y, a_pow, chunks_per_seq):
    g_, n_rows, cw = u_g.shape
    n_levels = a_pow.shape[1]
    return pl.pallas_call(
        functools.partial(_s5_kernel, n_levels, chunks_per_seq),
        grid=(g_,),
        in_specs=[pl.BlockSpec((1, n_rows, cw), lambda g: (g, 0, 0)),
                  pl.BlockSpec((1, cw, cw + 2 * C_STATE), lambda g: (g, 0, 0)),
                  pl.BlockSpec((1, 2 * C_STATE, cw), lambda g: (g, 0, 0)),
                  pl.BlockSpec((1, n_levels, 2 * C_STATE), lambda g: (g, 0, 0))],
        out_specs=[pl.BlockSpec((1, n_rows, cw), lambda g: (g, 0, 0)),
                   pl.BlockSpec((1, n_rows, 2 * C_STATE), lambda g: (g, 0, 0))],
        out_shape=[jax.ShapeDtypeStruct((g_, n_rows, cw), jnp.float32),
                   jax.ShapeDtypeStruct((g_, n_rows, 2 * C_STATE), jnp.float32)],
        compiler_params=_params(),
        name="s5_prompt",
    )(u_g, w_cat, w_y, a_pow)


def _s5_weights(lam_re, lam_im, log_dt, b_re, b_im, c_re, c_im, d_skip, n_levels):
    f32 = jnp.float32
    L = S5_CHUNK
    lam = lax.complex(lam_re.astype(f32), lam_im.astype(f32))
    dt = jnp.exp(log_dt.astype(f32))[:, None]
    a_bar = jnp.exp(lam * dt)
    b_bar = ((a_bar - 1.0) / lam)[..., None] * lax.complex(b_re.astype(f32), b_im.astype(f32))
    c_mat = lax.complex(c_re.astype(f32), c_im.astype(f32))
    taus = jnp.arange(L + 1, dtype=f32)
    a_pw = jnp.exp((lam * dt)[None] * taus[:, None, None])
    kern = jnp.einsum('gcp,tgp,gpd->tgcd', c_mat, a_pw[:L], b_bar).real
    t_idx = jnp.arange(L)
    diff = t_idx[None, :] - t_idx[:, None]
    toe = kern[jnp.clip(diff, 0, L - 1)]
    toe = jnp.where((diff >= 0)[:, :, None, None, None], toe, 0.0)
    eye = (diff == 0)[:, :, None, None, None] * (jnp.eye(C_GROUP, dtype=f32)[None, None, None] * d_skip.astype(f32)[None, None, :, :, None])
    toe = (toe + eye).transpose(2, 0, 4, 1, 3).reshape(C_GROUPS, L * C_GROUP, L * C_GROUP)
    e_c = a_pw[L - 1 - t_idx][:, :, :, None] * b_bar[None]
    e_c = e_c.transpose(1, 0, 3, 2).reshape(C_GROUPS, L * C_GROUP, C_STATE)
    w_cat = jnp.concatenate([toe, e_c.real, e_c.imag], axis=2)
    y_c = c_mat[None] * a_pw[1:L + 1][:, :, None, :]
    y_c = y_c.transpose(1, 3, 0, 2).reshape(C_GROUPS, C_STATE, L * C_GROUP)
    w_y = jnp.concatenate([y_c.real, -y_c.imag], axis=1)
    lv = (L * (2.0 ** jnp.arange(n_levels, dtype=f32)))
    a_lv = jnp.exp((lam * dt)[:, None, :] * lv[None, :, None])
    a_pow = jnp.concatenate([a_lv.real, a_lv.imag], axis=2)
    return w_cat.astype(jnp.bfloat16), w_y.astype(jnp.bfloat16), a_pow


def _outproj_kernel(x_ref, gate_ref, oa_ref, ob_ref, ys_ref, wglu_ref, bglu_ref, woa_ref, wob_ref, woc_ref, lng_ref, lnb_ref, o_ref):
    f32, bf16 = jnp.float32, jnp.bfloat16
    y = ys_ref[...]
    z = 0.5 * y * (1.0 + jnp.tanh(math.sqrt(2.0 / math.pi) * (y + 0.044715 * (y * y * y))))
    yc = z * jax.nn.sigmoid(jnp.dot(z.astype(bf16), wglu_ref[...], preferred_element_type=f32) + bglu_ref[...])
    mix = jnp.dot(oa_ref[...], woa_ref[...], preferred_element_type=f32)
    mix = mix + jnp.dot(ob_ref[...], wob_ref[...], preferred_element_type=f32)
    mix = mix + jnp.dot(yc.astype(bf16), woc_ref[...], preferred_element_type=f32)
    o_ref[...] = _ln(ALPHA * x_ref[...] + (1.0 + gate_ref[0]) * mix, lng_ref[...], lnb_ref[...])


def outproj(x, gate, oa, ob, ys, wglu, bglu, w_out, ln_g, ln_b, rows_per_group):
    n = x.shape[0]
    rows = min(ROWS, n)
    bf16 = jnp.bfloat16

    def rspec(w):
        return pl.BlockSpec((rows, w), lambda i: (i, 0))

    woa, wob, woc = w_out[:A_WIDTH].astype(bf16), w_out[A_WIDTH:A_WIDTH + B_WIDTH].astype(bf16), w_out[A_WIDTH + B_WIDTH:].astype(bf16)
    vec = _const_spec((1, D_MODEL))
    return pl.pallas_call(
        _outproj_kernel,
        grid=(n // rows,),
        in_specs=[rspec(D_MODEL), _mod_spec(gate, rows, rows_per_group), rspec(A_WIDTH), rspec(B_WIDTH), rspec(C_WIDTH),
                  _const_spec((C_WIDTH, C_WIDTH)), _const_spec((1, C_WIDTH)),
                  _const_spec(woa.shape), _const_spec(wob.shape), _const_spec(woc.shape), vec, vec],
        out_specs=rspec(D_MODEL),
        out_shape=jax.ShapeDtypeStruct((n, D_MODEL), jnp.float32),
        compiler_params=_params(),
        name="outproj",
    )(x, gate, oa, ob, ys, wglu.astype(bf16), bglu.reshape(1, C_WIDTH), woa, wob, woc, ln_g.reshape(1, D_MODEL), ln_b.reshape(1, D_MODEL))


def _prompt_trunk(x, mod, p):
    b_, t_ = x.shape[0], x.shape[1]
    n = b_ * t_
    xf = x.reshape(n, D_MODEL)
    pos = jnp.arange(t_, dtype=jnp.int32)
    cos, sin = _rope_tables(pos)
    cos, sin = jnp.tile(cos, (b_, 1)), jnp.tile(sin, (b_, 1))
    bias_near, bias_far = _bias_tables(p['rel_bias'])
    topk = min(TOPK_MAX, t_ // 4)
    n_chunks = t_ // S5_CHUNK
    n_levels = max(1, (n_chunks - 1).bit_length())
    outs = []
    for l in range(DEPTH):
        m = mod[l].reshape(b_, 3, 3, 1, D_MODEL)

        def mo(j, k):
            return m[:, j, k]

        xf = ffn_block(xf, mo(0, 0), mo(0, 1), mo(0, 2), *p['ffn_w'][l][0], p['ln_g'][l, 0], p['ln_b'][l, 0], t_,
                       ln_in=(p['ln_in_g'], p['ln_in_b']) if l == 0 else None)
        wp, wuq, wuk = p['mix_w'][l]
        aq, ak, av, kv, iq, misc, ik, ckv, kpe, kmla, qmla, cu = mixproj(
            xf, mo(1, 0), mo(1, 1), wp, p['g_qnorm'][l], p['g_kvnorm'][l], wuq, wuk, cos, sin, t_)

        def seq(a):
            return a.reshape(b_, t_, a.shape[-1])

        oa = idx_attn(seq(iq), seq(misc), seq(ik), seq(aq), seq(kv), bias_near, bias_far, topk)
        ob = mla_attn(seq(qmla), seq(kmla), p['wuv'][l])
        w_cat, w_y, a_pow = _s5_weights(p['ssm_lam_re'][l], p['ssm_lam_im'][l], p['ssm_log_dt'][l], p['ssm_b_re'][l], p['ssm_b_im'][l],
                                        p['ssm_c_re'][l], p['ssm_c_im'][l], p['ssm_d'][l], n_levels)
        u_g = cu.astype(jnp.bfloat16).reshape(b_ * n_chunks, S5_CHUNK, C_GROUPS, C_GROUP).transpose(2, 0, 1, 3).reshape(
            C_GROUPS, b_ * n_chunks, S5_CHUNK * C_GROUP)
        y_g, s_g = s5_prompt(u_g, w_cat, w_y, a_pow, n_chunks)
        ys = y_g.reshape(C_GROUPS, b_ * n_chunks, S5_CHUNK, C_GROUP).transpose(1, 2, 0, 3).reshape(n, C_WIDTH)
        s_last = s_g.reshape(C_GROUPS, b_, n_chunks, 2 * C_STATE)[:, :, -1].transpose(1, 0, 2)
        xf = outproj(xf, mo(1, 2), oa.reshape(n, A_WIDTH), ob.reshape(n, B_WIDTH), ys, p['w_glu'][l], p['b_glu'][l], p['w_out'][l],
                     p['ln_g'][l, 1], p['ln_b'][l, 1], t_)
        xf = ffn_block(xf, mo(2, 0), mo(2, 1), mo(2, 2), *p['ffn_w'][l][1], p['ln_g'][l, 2], p['ln_b'][l, 2], t_)
        outs.append((ak.reshape(b_, t_, A_KV_HEADS, HEAD_DIM), av.reshape(b_, t_, A_KV_HEADS, HEAD_DIM),
                     seq(misc)[:, :, :IDX_DIM], seq(ckv), seq(kpe)[:, :, :ROPE_DIM], s_last[:, :, :C_STATE], s_last[:, :, C_STATE:]))
    stacked = [jnp.stack(s, axis=0) for s in zip(*outs)]
    return xf.reshape(b_, t_, D_MODEL), stacked


GROUP_PAGES = 16
GK = GROUP_PAGES * PAGE_SIZE
HROWS = 8


def _sample_attn_kernel(layer, topk, n_groups, pt_ref, iq_ref, iw_ref, aq_ref, qm_ref, ikn_ref, kvn_ref, kmn_ref, bias_ref, wuv_ref,
                        idx_hbm, lat_hbm, rope_hbm, ak_hbm, av_hbm, oa_ref, ob_ref,
                        idx_buf, lat_buf, rope_buf, ak_buf, av_buf, keys_ref, sem_i, sem_a):
    f32, bf16, i32 = jnp.float32, jnp.bfloat16, jnp.int32
    b = pl.program_id(0)
    past = n_groups * GK
    nt = (((1,), (1,)), ((), ()))
    neg_key = jnp.int32(INT_MIN + 1)

    def copies(pools_bufs, sem, g, slot):
        out = []
        for j in range(GROUP_PAGES):
            phys = pt_ref[b, g * GROUP_PAGES + j]
            for pool, buf in pools_bufs:
                out.append(pltpu.make_async_copy(pool.at[layer, phys], buf.at[slot, pl.ds(j * PAGE_SIZE, PAGE_SIZE)], sem.at[slot]))
        return out

    def pipeline(pools_bufs, sem, compute, init):
        for c in copies(pools_bufs, sem, 0, 0):
            c.start()

        def body(g, carry):
            slot = g % 2

            @pl.when(g + 1 < n_groups)
            def _():
                for c in copies(pools_bufs, sem, g + 1, 1 - slot):
                    c.start()

            for c in copies(pools_bufs, sem, g, slot):
                c.wait()
            return compute(g, slot, carry)

        return lax.fori_loop(0, n_groups, body, init)

    def to_key(sc):
        bits = lax.bitcast_convert_type(sc + 0.0, i32)
        return jnp.where(bits < 0, bits ^ 0x7FFFFFFF, bits)

    iq = iq_ref[0]
    iw = iw_ref[0]
    keys_ref[...] = jnp.full(keys_ref.shape, neg_key, i32)

    def score_group(g, slot, carry):
        kb = idx_buf[slot].astype(bf16)
        s = lax.dot_general(iq[:, :IDX_DIM], kb, nt, preferred_element_type=f32)
        sc = jnp.sum(jnp.maximum(s, 0.0) * iw, axis=0, keepdims=True)
        keys_ref[pl.ds(g, 1), :] = to_key(sc)
        return carry

    pipeline([(idx_hbm, idx_buf)], sem_i, score_group, 0)
    s_new = jnp.sum(iq.astype(f32) * ikn_ref[0].astype(f32), axis=1, keepdims=True)
    key_new = to_key(jnp.sum(jnp.maximum(s_new, 0.0) * iw, axis=0, keepdims=True))

    rows = keys_ref.shape[0]
    idx_all = lax.broadcasted_iota(i32, (rows, GK), 0) * GK + lax.broadcasted_iota(i32, (rows, GK), 1)

    def count(pred):
        k = keys_ref[...]
        c = jnp.sum(jnp.where(pred(k, idx_all), 1.0, 0.0), axis=1, keepdims=True)
        return jnp.sum(c, axis=0, keepdims=True) + jnp.where(pred(key_new, jnp.int32(past)), 1.0, 0.0)

    kf = float(topk)
    ans0 = jnp.where(count(lambda k, idx: k >= 0) >= kf, 0, INT_MIN).astype(i32)

    def bit_step(i, ans):
        cand = ans | jnp.left_shift(jnp.int32(1), 30 - i)
        return jnp.where(count(lambda k, idx: k >= cand) >= kf, cand, ans)

    tau = lax.fori_loop(0, 31, bit_step, ans0)
    need = kf - count(lambda k, idx: k > tau)
    idx_bits = (past + 1).bit_length()

    def tie_step(i, x):
        cand = x | jnp.left_shift(jnp.int32(1), idx_bits - 1 - i)
        return jnp.where(count(lambda k, idx: (k == tau) & (idx < cand)) < need, cand, x)

    j_cut = lax.fori_loop(0, idx_bits, tie_step, jnp.zeros((1, 1), i32))

    def selected(k, idx):
        return (k > tau) | ((k == tau) & (idx <= j_cut))

    aq = aq_ref[0]
    qm = qm_ref[0]

    def online(carry, s, v):
        m, l, acc = carry
        m_new = jnp.maximum(m, jnp.max(s, axis=-1, keepdims=True))
        m_safe = jnp.where(m_new == -jnp.inf, 0.0, m_new)
        p = jnp.exp(s - m_safe)
        corr = jnp.exp(m - m_safe)
        l = corr * l + jnp.sum(p, axis=-1, keepdims=True)
        acc = corr * acc + jnp.dot(p.astype(bf16), v, preferred_element_type=f32)
        return m_new, l, acc

    def attend_group(g, slot, carry):
        ca, cm = carry
        base = pl.multiple_of(g * GK, GK)
        k = keys_ref[pl.ds(g, 1), :]
        idx = base + lax.broadcasted_iota(i32, (1, GK), 1)
        sa = lax.dot_general(aq, ak_buf[slot].astype(bf16), nt, preferred_element_type=f32) + bias_ref[:, pl.ds(base, GK)]
        ca = online(ca, jnp.where(selected(k, idx), sa, -jnp.inf), av_buf[slot].astype(bf16))
        lat = lat_buf[slot].astype(bf16)
        sm = lax.dot_general(qm[:, :KV_RANK], lat, nt, preferred_element_type=f32)
        sm = sm + lax.dot_general(qm[:, KV_RANK:KV_RANK + ROPE_DIM], rope_buf[slot].astype(bf16), nt, preferred_element_type=f32)
        cm = online(cm, sm * MLA_SCALE, lat)
        return ca, cm

    def init():
        return (jnp.full((HROWS, 1), -jnp.inf, f32), jnp.zeros((HROWS, 1), f32), jnp.zeros((HROWS, LANES), f32))

    ca, cm = pipeline([(ak_hbm, ak_buf), (av_hbm, av_buf), (lat_hbm, lat_buf), (rope_hbm, rope_buf)], sem_a, attend_group, (init(), init()))

    def finish(carry, s_new, v_new):
        m, l, acc = carry
        m_new = jnp.maximum(m, s_new)
        m_safe = jnp.where(m_new == -jnp.inf, 0.0, m_new)
        p = jnp.exp(s_new - m_safe)
        corr = jnp.exp(m - m_safe)
        return (corr * acc + p * v_new) / (corr * l + p)

    kvn = kvn_ref[0].astype(f32)
    sa_new = jnp.sum(aq.astype(f32) * kvn[:, :LANES], axis=1, keepdims=True) + bias_ref[:, past:past + 1]
    sa_new = jnp.where(selected(key_new, jnp.int32(past)), sa_new, -jnp.inf)
    out_a = finish(ca, sa_new, kvn[:, LANES:].astype(bf16).astype(f32))
    kmn = kmn_ref[0].astype(f32)
    sm_new = jnp.sum(qm.astype(f32) * kmn, axis=1, keepdims=True) * MLA_SCALE
    out_m = finish(cm, sm_new, kmn[:, :KV_RANK])
    lane = lax.broadcasted_iota(i32, (1, LANES), 1)
    blk_a = jnp.where(lane < HEAD_DIM, out_a[0:1], pltpu.roll(out_a[1:2], HEAD_DIM, 1))
    blk_b = jnp.where(lane < HEAD_DIM, pltpu.roll(out_a[2:3], HEAD_DIM, 1), out_a[3:4])
    oa_ref[0] = jnp.concatenate([blk_a, blk_b], axis=1).astype(oa_ref.dtype)
    o_lat = out_m.astype(bf16)
    ob = jnp.zeros((1, B_WIDTH), f32)
    for hd in range(B_HEADS):
        ob = ob + jnp.dot(o_lat, wuv_ref[hd], preferred_element_type=f32)[hd:hd + 1]
    ob_ref[0] = ob.astype(ob_ref.dtype)


def sample_attn(layer, page_table, iq, iw, aq, qm, ikn, kvn, kmn, bias, wuv, c_idx, c_lat, c_rope, c_ak, c_av, topk):
    n_seq, n_pages = page_table.shape
    assert n_pages % GROUP_PAGES == 0
    n_groups = n_pages // GROUP_PAGES
    rows = -(-n_groups // HROWS) * HROWS
    f32 = jnp.float32

    def sspec(r, w):
        return pl.BlockSpec((1, r, w), lambda b, pt: (b, 0, 0))

    def cspec(shape):
        nd = len(shape)
        return pl.BlockSpec(shape, lambda b, pt: (0,) * nd, pipeline_mode=pl.Buffered(1))

    hbm = pl.BlockSpec(memory_space=pl.ANY)
    return pl.pallas_call(
        functools.partial(_sample_attn_kernel, layer, topk, n_groups),
        grid_spec=pltpu.PrefetchScalarGridSpec(
            num_scalar_prefetch=1,
            grid=(n_seq,),
            in_specs=[sspec(HROWS, LANES), sspec(HROWS, 1), sspec(HROWS, LANES), sspec(HROWS, 2 * LANES),
                      sspec(1, LANES), sspec(1, 2 * LANES), sspec(1, 2 * LANES), cspec(bias.shape), cspec(wuv.shape),
                      hbm, hbm, hbm, hbm, hbm],
            out_specs=[sspec(1, A_WIDTH), sspec(1, B_WIDTH)],
            scratch_shapes=[pltpu.VMEM((2, GK, IDX_DIM), f32), pltpu.VMEM((2, GK, KV_RANK), f32), pltpu.VMEM((2, GK, ROPE_DIM), f32),
                            pltpu.VMEM((2, GK, LANES), f32), pltpu.VMEM((2, GK, LANES), f32), pltpu.VMEM((rows, GK), jnp.int32),
                            pltpu.SemaphoreType.DMA((2,)), pltpu.SemaphoreType.DMA((2,))]),
        out_shape=[jax.ShapeDtypeStruct((n_seq, 1, A_WIDTH), jnp.bfloat16), jax.ShapeDtypeStruct((n_seq, 1, B_WIDTH), jnp.bfloat16)],
        compiler_params=_params(),
        name="sample_attn",
    )(page_table, iq, iw, aq, qm, ikn, kvn, kmn, bias, wuv, c_idx, c_lat, c_rope, c_ak, c_av)


def _s5_step_kernel(u_ref, sre_ref, sim_ref, are_ref, aim_ref, bre_ref, bim_ref, cre_ref, cim_ref, d_ref, y_ref, ore_ref, oim_ref):
    hi = lax.Precision.HIGHEST
    u = u_ref[...]
    s_re, s_im, a_re, a_im = sre_ref[...], sim_ref[...], are_ref[...], aim_ref[...]
    n_re = a_re * s_re - a_im * s_im + jnp.dot(u, bre_ref[...], precision=hi, preferred_element_type=jnp.float32)
    n_im = a_re * s_im + a_im * s_re + jnp.dot(u, bim_ref[...], precision=hi, preferred_element_type=jnp.float32)
    ore_ref[...] = n_re
    oim_ref[...] = n_im
    y_ref[...] = (jnp.dot(n_re, cre_ref[...], precision=hi, preferred_element_type=jnp.float32)
                  - jnp.dot(n_im, cim_ref[...], precision=hi, preferred_element_type=jnp.float32) + d_ref[...] * u)


def s5_step(u, s_re, s_im, ops):
    n = u.shape[0]
    n_state = C_GROUPS * C_STATE
    f32 = jnp.float32
    args = (u, s_re, s_im) + tuple(ops)
    return pl.pallas_call(
        _s5_step_kernel,
        grid=(1,),
        in_specs=[_const_spec(a.shape) for a in args],
        out_specs=[pl.BlockSpec((n, C_WIDTH), lambda i: (0, 0)), pl.BlockSpec((n, n_state), lambda i: (0, 0)),
                   pl.BlockSpec((n, n_state), lambda i: (0, 0))],
        out_shape=[jax.ShapeDtypeStruct((n, C_WIDTH), f32), jax.ShapeDtypeStruct((n, n_state), f32), jax.ShapeDtypeStruct((n, n_state), f32)],
        compiler_params=_params(),
        name="s5_step",
    )(*args)


def _s5_step_weights(lam_re, lam_im, log_dt, b_re, b_im, c_re, c_im, d_skip):
    f32 = jnp.float32
    lam = lax.complex(lam_re.astype(f32), lam_im.astype(f32))
    dt = jnp.exp(log_dt.astype(f32))[:, None]
    a_bar = jnp.exp(lam * dt)
    b_bar = ((a_bar - 1.0) / lam)[..., None] * lax.complex(b_re.astype(f32), b_im.astype(f32))
    c_mat = lax.complex(c_re.astype(f32), c_im.astype(f32))
    eye = jnp.eye(C_GROUPS, dtype=f32)

    def bdiag_in(m):
        return jnp.einsum('gpc,gh->gchp', m, eye).reshape(C_WIDTH, C_GROUPS * C_STATE)

    def bdiag_out(m):
        return jnp.einsum('gcp,gh->gphc', m, eye).reshape(C_GROUPS * C_STATE, C_WIDTH)

    return (a_bar.real.reshape(1, -1), a_bar.imag.reshape(1, -1), bdiag_in(b_bar.real), bdiag_in(b_bar.imag),
            bdiag_out(c_mat.real), bdiag_out(c_mat.imag), d_skip.astype(f32).reshape(1, C_WIDTH))


def _sample_trunk(x, mod, p, page_table, caches, state_re, state_im):
    n = x.shape[0]
    n_pages = page_table.shape[1]
    past = n_pages * PAGE_SIZE
    xs = x.reshape(n, D_MODEL)
    cos, sin = _rope_tables(jnp.full((n,), past, jnp.int32))
    dist = past - jnp.arange(past + LANES, dtype=jnp.int32)
    bias = p['rel_bias'][_t5_bucket(dist)].astype(jnp.float32).T
    bias = jnp.concatenate([bias, jnp.zeros((HROWS - A_HEADS, past + LANES), jnp.float32)], axis=0)
    topk = min(TOPK_MAX, (past + 1) // 4)
    c_idx, c_lat, c_rope, c_ak, c_av = caches
    outs = []

    def head_rows(a, w):
        a = a.reshape(n, -1, w)
        return jnp.concatenate([a, jnp.zeros((n, HROWS - a.shape[1], w), a.dtype)], axis=1)

    for l in range(DEPTH):
        m = mod[l].reshape(n, 3, 3, D_MODEL)

        def mo(j, k):
            return m[:, j, k].reshape(1, n, D_MODEL)

        xs = ffn_block(xs, mo(0, 0), mo(0, 1), mo(0, 2), *p['ffn_w'][l][0], p['ln_g'][l, 0], p['ln_b'][l, 0], 1,
                       ln_in=(p['ln_in_g'], p['ln_in_b']) if l == 0 else None)
        wp, wuq, wuk = p['mix_w'][l]
        aq, ak, av, kv, iq, misc, ik, ckv, kpe, kmla, qmla, cu = mixproj(
            xs, mo(1, 0), mo(1, 1), wp, p['g_qnorm'][l], p['g_kvnorm'][l], wuq, wuk, cos, sin, 1)
        iw = head_rows(misc[:, IDX_DIM:IDX_DIM + IDX_HEADS], 1)
        oa, ob = sample_attn(l, page_table, head_rows(iq, LANES), iw, head_rows(aq, LANES), head_rows(qmla, 2 * LANES),
                             ik.reshape(n, 1, LANES), kv.reshape(n, 1, 2 * LANES), kmla.reshape(n, 1, 2 * LANES), bias, p['wuv'][l],
                             c_idx, c_lat, c_rope, c_ak, c_av, topk)
        ops = _s5_step_weights(p['ssm_lam_re'][l], p['ssm_lam_im'][l], p['ssm_log_dt'][l], p['ssm_b_re'][l], p['ssm_b_im'][l],
                               p['ssm_c_re'][l], p['ssm_c_im'][l], p['ssm_d'][l])
        ys, s_re, s_im = s5_step(cu, state_re[l].reshape(n, -1), state_im[l].reshape(n, -1), ops)
        xs = outproj(xs, mo(1, 2), oa.reshape(n, A_WIDTH), ob.reshape(n, B_WIDTH), ys, p['w_glu'][l], p['b_glu'][l], p['w_out'][l],
                     p['ln_g'][l, 1], p['ln_b'][l, 1], 1)
        xs = ffn_block(xs, mo(2, 0), mo(2, 1), mo(2, 2), *p['ffn_w'][l][1], p['ln_g'][l, 2], p['ln_b'][l, 2], 1)
        outs.append((ak.reshape(n, 1, A_KV_HEADS, HEAD_DIM), av.reshape(n, 1, A_KV_HEADS, HEAD_DIM), misc[:, None, :IDX_DIM], ckv[:, None, :],
                     kpe[:, None, :ROPE_DIM], s_re.reshape(n, C_GROUPS, C_STATE), s_im.reshape(n, C_GROUPS, C_STATE)))
    stacked = [jnp.stack(s, axis=0) for s in zip(*outs)]
    return xs.reshape(n, 1, D_MODEL), stacked


def _layer_norm(x, g, b):
    xf = x.astype(jnp.float32)
    mu = jnp.mean(xf, -1, keepdims=True)
    var = jnp.mean(jnp.square(xf - mu), -1, keepdims=True)
    return ((xf - mu) * lax.rsqrt(var + LN_EPS) * g + b).astype(x.dtype)


def _rms_norm(x, g):
    xf = x.astype(jnp.float32)
    return (xf * lax.rsqrt(jnp.mean(xf * xf, -1, keepdims=True) + 1e-6) * g).astype(x.dtype)


def _rope(x, pos):
    half = ROPE_DIM // 2
    freqs = ROPE_THETA ** (-jnp.arange(half, dtype=jnp.float32) / half)
    ang = pos.astype(jnp.float32)[:, None] * freqs
    cos = jnp.cos(ang)[:, None, :]
    sin = jnp.sin(ang)[:, None, :]
    x1 = x[..., :half].astype(jnp.float32)
    x2 = x[..., half:].astype(jnp.float32)
    return jnp.concatenate([x1 * cos - x2 * sin, x1 * sin + x2 * cos], -1).astype(x.dtype)


def _gather_rows(rows, sel):
    return jax.vmap(lambda r, i: r[i])(rows, sel)


def _gather_paged_all(pool, l, page_table):
    g = pool[l, page_table]
    return g.reshape(g.shape[0], g.shape[1] * g.shape[2], *g.shape[3:])


def _gather_paged_rows(pool, l, page_table, pos):
    db = page_table.shape[0]
    phys = jnp.take_along_axis(page_table, (pos // PAGE_SIZE).reshape(db, -1), axis=1).reshape(pos.shape)
    return pool[l, phys, pos % PAGE_SIZE]


def _project_groups(h, pos, w_in, g_qn, g_kvn, w_uq, w_uk):
    b_, t_ = h.shape[0], h.shape[1]
    p = jnp.einsum('btd,de->bte', h, w_in)
    split_at = [int(s) for s in np.cumsum(IN_SPLITS)[:-1]]
    a_q, a_k, a_v, i_q, i_w, i_k, b_cq, b_ckv, b_kpe, c_u = jnp.split(p, split_at, axis=-1)
    a_q = a_q.reshape(b_, t_, A_HEADS, HEAD_DIM)
    a_k = a_k.reshape(b_, t_, A_KV_HEADS, HEAD_DIM)
    a_v = a_v.reshape(b_, t_, A_KV_HEADS, HEAD_DIM)
    i_q = i_q.reshape(b_, t_, IDX_HEADS, IDX_DIM) * (IDX_DIM ** -0.5)
    i_w = i_w * (IDX_HEADS ** -0.5)
    cq = _rms_norm(b_cq, g_qn)
    q = jnp.einsum('btr,rhe->bthe', cq, w_uq)
    q_pe = _rope(q[..., NOPE_DIM:], pos)
    q_lat = jnp.einsum('bthn,rhn->bthr', q[..., :NOPE_DIM], w_uk)
    ckv = _rms_norm(b_ckv, g_kvn)
    kpe = _rope(b_kpe[:, :, None, :], pos)[:, :, 0]
    return a_q, a_k, a_v, i_q, i_w, i_k, q_lat, q_pe, ckv, kpe, c_u


def _indexer_topk(q_idx, w_idx, k_idx, q_pos, k_pos, topk):
    s = jax.nn.relu(jnp.einsum('bthd,bsd->bths', q_idx, k_idx).astype(jnp.float32))
    score = jnp.einsum('bths,bth->bts', s, w_idx.astype(jnp.float32))
    score = jnp.where(k_pos[None, None, :] <= q_pos[None, :, None], score, -jnp.inf)
    _, sel = lax.top_k(score, topk)
    return sel


def _sparse_attend(q, k_sel, v_sel, sel, q_pos, rel_bias):
    b_, t_, n_sel = sel.shape
    grp = A_HEADS // A_KV_HEADS
    qg = q.reshape(b_, t_, A_KV_HEADS, grp, HEAD_DIM)
    logits = jnp.einsum('btcgd,btncd->btcgn', qg, k_sel).astype(jnp.float32) * (HEAD_DIM ** -0.5)
    dist = q_pos[None, :, None] - sel
    bias = rel_bias[_t5_bucket(dist)].astype(jnp.float32)
    bias = bias.reshape(b_, t_, n_sel, A_KV_HEADS, grp).transpose(0, 1, 3, 4, 2)
    logits = jnp.where((dist >= 0)[:, :, None, None, :], logits + bias, -jnp.inf)
    p = jax.nn.softmax(logits, axis=-1).astype(v_sel.dtype)
    o = jnp.einsum('btcgn,btncd->btcgd', p, v_sel)
    return o.reshape(b_, t_, A_WIDTH)


def _mla_attend(q_lat, q_pe, ckv, kpe, q_pos, k_pos):
    logits = (jnp.einsum('bthr,bsr->bhts', q_lat, ckv) + jnp.einsum('bthe,bse->bhts', q_pe, kpe)).astype(jnp.float32) * MLA_SCALE
    logits = jnp.where(k_pos[None, None, None, :] <= q_pos[None, None, :, None], logits, -jnp.inf)
    p = jax.nn.softmax(logits, axis=-1).astype(ckv.dtype)
    return jnp.einsum('bhts,bsr->bthr', p, ckv)


def _s5_step(u, s0_re, s0_im, lam_re, lam_im, log_dt, b_re, b_im, c_re, c_im, d_skip, w_glu, b_glu):
    f32 = jnp.float32
    b_ = u.shape[0]
    lam = lax.complex(lam_re.astype(f32), lam_im.astype(f32))
    dt = jnp.exp(log_dt.astype(f32))[:, None]
    a_bar = jnp.exp(lam * dt)
    b_bar = ((a_bar - 1.0) / lam)[..., None] * lax.complex(b_re.astype(f32), b_im.astype(f32))
    c_mat = lax.complex(c_re.astype(f32), c_im.astype(f32))
    ug = u.astype(f32).reshape(b_, 1, C_GROUPS, C_GROUP)
    bu = jnp.einsum('gpc,btgc->btgp', b_bar, ug.astype(jnp.complex64))
    s0 = lax.complex(s0_re.astype(f32), s0_im.astype(f32))
    s = bu + a_bar * s0[:, None]
    y = jnp.einsum('gcp,btgp->btgc', c_mat, s).real + d_skip.astype(f32) * ug
    z = jax.nn.gelu(y.reshape(b_, 1, C_WIDTH))
    out = z * jax.nn.sigmoid(jnp.einsum('bte,ef->btf', z, w_glu) + b_glu)
    s_last = s[:, -1]
    return out.astype(u.dtype), jnp.real(s_last), jnp.imag(s_last)


def kernel(x_prompt, x_sample, c_prompt, c_sample, page_table, cache_attn_k, cache_attn_v, cache_idx_k, cache_mla_latent, cache_mla_rope, state_ssm_re, state_ssm_im, ln_in_g, ln_in_b, rel_bias, w_ada, b_ada, ln_g, ln_b, w_ffn_in, w_ffn_out, w_in, w_out, g_qnorm, g_kvnorm, w_uq, w_uk, w_uv, ssm_lam_re, ssm_lam_im, ssm_log_dt, ssm_b_re, ssm_b_im, ssm_c_re, ssm_c_im, ssm_d, w_glu, b_glu):
    n_p, n_s = c_prompt.shape[0], c_sample.shape[0]
    n_c = -(-(n_p + n_s) // 8) * 8
    c_all = jnp.concatenate([c_prompt, c_sample, jnp.zeros((n_c - n_p - n_s, D_MODEL), jnp.float32)], axis=0)
    mod_all = adaln(c_all, w_ada, b_ada)
    params = dict(
        ffn_w=[[_ffn_weights(w_ffn_in[l, j], w_ffn_out[l, j]) for j in range(2)] for l in range(DEPTH)],
        mix_w=[_mix_weights(w_in[l], w_uq[l], w_uk[l]) for l in range(DEPTH)],
        wuv=[_wuv_blocks(w_uv[l]) for l in range(DEPTH)],
        ln_in_g=ln_in_g, ln_in_b=ln_in_b, ln_g=ln_g, ln_b=ln_b, rel_bias=rel_bias, g_qnorm=g_qnorm, g_kvnorm=g_kvnorm,
        ssm_lam_re=ssm_lam_re, ssm_lam_im=ssm_lam_im, ssm_log_dt=ssm_log_dt, ssm_b_re=ssm_b_re, ssm_b_im=ssm_b_im,
        ssm_c_re=ssm_c_re, ssm_c_im=ssm_c_im, ssm_d=ssm_d, w_glu=w_glu, b_glu=b_glu, w_out=w_out)
    y_prompt, st_p = _prompt_trunk(x_prompt, mod_all[:, :n_p], params)
    n_pool = cache_attn_k.shape[1]
    caches = (cache_idx_k, cache_mla_latent, cache_mla_rope, cache_attn_k.reshape(DEPTH, n_pool, PAGE_SIZE, LANES),
              cache_attn_v.reshape(DEPTH, n_pool, PAGE_SIZE, LANES))
    y_sample, st_s = _sample_trunk(x_sample, mod_all[:, n_p:n_p + n_s], params, page_table, caches, state_ssm_re, state_ssm_im)
    return (y_prompt, y_sample, st_p[0], st_s[0], st_p[1], st_s[1], st_p[2], st_s[2], st_p[3], st_s[3], st_p[4], st_s[4], st_p[5], st_s[5], st_p[6], st_s[6])

    def ssm_layer(u, s0_re, s0_im, l):
        return _s5_step(u, s0_re, s0_im, ssm_lam_re[l], ssm_lam_im[l], ssm_log_dt[l], ssm_b_re[l], ssm_b_im[l], ssm_c_re[l], ssm_c_im[l], ssm_d[l], w_glu[l], b_glu[l])

    def mla_out(o_lat, l):
        o = jnp.einsum('bthr,rhv->bthv', o_lat, w_uv[l])
        return o.reshape(o.shape[0], o.shape[1], B_WIDTH)

    def sample_mix(h, l):
        n_dec = h.shape[1]
        past = page_table.shape[1] * PAGE_SIZE
        pos = past + jnp.arange(n_dec, dtype=jnp.int32)
        k_pos = jnp.arange(past + n_dec, dtype=jnp.int32)
        a_q, a_k, a_v, i_q, i_w, i_k, q_lat, q_pe, ckv, kpe, u = _project_groups(h, pos, w_in[l], g_qnorm[l], g_kvnorm[l], w_uq[l], w_uk[l])
        topk = min(TOPK_MAX, (past + n_dec) // 4)
        i_k_all = jnp.concatenate([_gather_paged_all(cache_idx_k, l, page_table), i_k], axis=1)
        sel = _indexer_topk(i_q, i_w, i_k_all, pos, k_pos, topk)
        is_past = sel < past
        sel_past = jnp.minimum(sel, past - 1)
        sel_new = jnp.clip(sel - past, 0, n_dec - 1)
        k_sel = jnp.where(is_past[..., None, None], _gather_paged_rows(cache_attn_k, l, page_table, sel_past), _gather_rows(a_k, sel_new))
        v_sel = jnp.where(is_past[..., None, None], _gather_paged_rows(cache_attn_v, l, page_table, sel_past), _gather_rows(a_v, sel_new))
        oa = _sparse_attend(a_q, k_sel, v_sel, sel, pos, rel_bias)
        ckv_all = jnp.concatenate([_gather_paged_all(cache_mla_latent, l, page_table), ckv], axis=1)
        kpe_all = jnp.concatenate([_gather_paged_all(cache_mla_rope, l, page_table), kpe], axis=1)
        ob = _mla_attend(q_lat, q_pe, ckv_all, kpe_all, pos, k_pos)
        y_c, s_re, s_im = ssm_layer(u, state_ssm_re[l], state_ssm_im[l], l)
        mixed = jnp.concatenate([oa, mla_out(ob, l), y_c], axis=-1)
        return mixed, (a_k, a_v, i_k, ckv, kpe, s_re, s_im)

    mod_s = mod_all[:, n_p:n_p + n_s]
    xs = x_sample.reshape(n_s, D_MODEL)
    states = []
    for l in range(DEPTH):
        m = mod_s[l].reshape(n_s, 3, 3, D_MODEL)

        def mo(j, k):
            return m[:, j, k].reshape(1, n_s, D_MODEL)

        xs = ffn_block(xs, mo(0, 0), mo(0, 1), mo(0, 2), *params['ffn_w'][l][0], ln_g[l, 0], ln_b[l, 0], 1,
                       ln_in=(ln_in_g, ln_in_b) if l == 0 else None)
        xx = xs.reshape(n_s, 1, D_MODEL)
        mixed, st = sample_mix(xx * (1.0 + m[:, 1, 1][:, None]) + m[:, 1, 0][:, None], l)
        y = jnp.einsum('bte,ed->btd', mixed, w_out[l])
        xx = _layer_norm(ALPHA * xx + (1.0 + m[:, 1, 2][:, None]) * y, ln_g[l, 1], ln_b[l, 1])
        xs = ffn_block(xx.reshape(n_s, D_MODEL), mo(2, 0), mo(2, 1), mo(2, 2), *params['ffn_w'][l][1], ln_g[l, 2], ln_b[l, 2], 1)
        states.append(st)
    st_s = [jnp.stack(s, axis=0) for s in zip(*states)]
    y_sample = xs.reshape(n_s, 1, D_MODEL)
    return (y_prompt, y_sample, st_p[0], st_s[0], st_p[1], st_s[1], st_p[2], st_s[2], st_p[3], st_s[3], st_p[4], st_s[4], st_p[5], st_s[5], st_p[6], st_s[6])
```

```python
import functools
import math

import jax
import jax.numpy as jnp
import numpy as np
from jax import lax
from jax.experimental import pallas as pl
from jax.experimental.pallas import tpu as pltpu

D_MODEL = 1024
BATCH = 2
SEQ = 8192
DEPTH = 2
DEC_BATCH = 128
DEC_SEQ = 1
PAST_LEN = 16384
PAGE_SIZE = 128
HEAD_DIM = 64
A_HEADS = 4
A_KV_HEADS = 2
IDX_HEADS = 4
IDX_DIM = 32
TOPK_MAX = 256
REL_BUCKETS = 32
REL_MAX_DIST = 128
B_HEADS = 4
Q_RANK = 256
KV_RANK = 128
NOPE_DIM = 64
ROPE_DIM = 32
V_DIM = 64
ROPE_THETA = 10000.0
A_WIDTH = A_HEADS * HEAD_DIM
B_WIDTH = B_HEADS * V_DIM
C_WIDTH = D_MODEL - A_WIDTH - B_WIDTH
C_GROUP = 16
C_GROUPS = C_WIDTH // C_GROUP
C_STATE = 64
D_FF = 2816
Q_BLOCK = 128
LN_EPS = 1e-5
ALPHA = (2.0 * DEPTH) ** 0.25
MLA_SCALE = (NOPE_DIM + ROPE_DIM) ** -0.5
IN_SPLITS = (A_HEADS * HEAD_DIM, A_KV_HEADS * HEAD_DIM, A_KV_HEADS * HEAD_DIM, IDX_HEADS * IDX_DIM, IDX_HEADS, IDX_DIM, Q_RANK, KV_RANK, ROPE_DIM, C_WIDTH)
D_IN = sum(IN_SPLITS)

VMEM_LIMIT_V7X = 56 * 1024 * 1024
LANES = 128
FF_CHUNK = 256
N_FF_CHUNKS = D_FF // FF_CHUNK
ROWS = 512
TQ = 128
TK = 512
S5_CHUNK = 16
INT_MIN = -2 ** 31

COL_AQ = 0
COL_AK = COL_AQ + A_HEADS * LANES
COL_AV = COL_AK + LANES
COL_IQ = COL_AV + LANES
COL_CQ = COL_IQ + IDX_HEADS * LANES
COL_CKV = COL_CQ + Q_RANK
COL_CU = COL_CKV + KV_RANK
COL_KA = COL_CU + C_WIDTH
COL_KB = COL_KA + LANES
COL_MISC = COL_KB + LANES
N_PROJ = COL_MISC + LANES
UQ_NOPE = 0
UQ_A = B_HEADS * NOPE_DIM
UQ_B = UQ_A + B_HEADS * LANES
N_UQ = UQ_B + B_HEADS * LANES


def _params(n_axes=1):
    return pltpu.CompilerParams(dimension_semantics=("arbitrary",) * n_axes, vmem_limit_bytes=VMEM_LIMIT_V7X)


def _ln(v, g, b):
    mu = jnp.mean(v, -1, keepdims=True)
    d = v - mu
    var = jnp.mean(d * d, -1, keepdims=True)
    return d * lax.rsqrt(var + LN_EPS) * g + b


def _mod_spec(mod, rows, rows_per_group):
    if mod.shape[1] == 1:
        return pl.BlockSpec((1, 1, D_MODEL), lambda i: (i // (rows_per_group // rows), 0, 0))
    return pl.BlockSpec((1, rows, D_MODEL), lambda i: (0, i, 0))


def _const_spec(shape):
    nd = len(shape)
    return pl.BlockSpec(shape, lambda *_: (0,) * nd, pipeline_mode=pl.Buffered(1))


def _to_key(score):
    bits = lax.bitcast_convert_type(score + 0.0, jnp.int32)
    return jnp.where(bits < 0, bits ^ 0x7FFFFFFF, bits)


ADA_COLS = 1152


def _adaln_kernel(c_ref, w_ref, b_ref, o_ref):
    c = c_ref[...]
    o_ref[0] = jnp.dot(c * jax.nn.sigmoid(c), w_ref[0], preferred_element_type=jnp.float32) + b_ref[0]


def adaln(c, w_ada, b_ada):
    n = c.shape[0]
    n_out = w_ada.shape[2]
    return pl.pallas_call(
        _adaln_kernel,
        grid=(DEPTH, n_out // ADA_COLS),
        in_specs=[pl.BlockSpec((n, D_MODEL), lambda l, j: (0, 0)),
                  pl.BlockSpec((1, D_MODEL, ADA_COLS), lambda l, j: (l, 0, j)),
                  pl.BlockSpec((1, 1, ADA_COLS), lambda l, j: (l, 0, j))],
        out_specs=pl.BlockSpec((1, n, ADA_COLS), lambda l, j: (l, 0, j)),
        out_shape=jax.ShapeDtypeStruct((DEPTH, n, n_out), jnp.float32),
        compiler_params=_params(2),
        name="adaln",
    )(c, w_ada, b_ada.reshape(DEPTH, 1, n_out))


def _ffn_kernel(pre_ln, x_ref, shift_ref, scale_ref, gate_ref, wg_ref, wu_ref, wd_ref, lng_ref, lnb_ref, *rest):
    if pre_ln:
        ing_ref, inb_ref, o_ref, acc_ref = rest
    else:
        o_ref, acc_ref = rest
    x = x_ref[...]
    if pre_ln:
        x = _ln(x, ing_ref[...], inb_ref[...])
    h = (x * (1.0 + scale_ref[0]) + shift_ref[0]).astype(jnp.bfloat16)
    acc_ref[...] = jnp.zeros_like(acc_ref)

    def chunk(c, carry):
        g = jnp.dot(h, wg_ref[c], preferred_element_type=jnp.float32)
        u = jnp.dot(h, wu_ref[c], preferred_element_type=jnp.float32)
        act = (g * jax.nn.sigmoid(g) * u).astype(jnp.bfloat16)
        acc_ref[...] += jnp.dot(act, wd_ref[c], preferred_element_type=jnp.float32)
        return carry

    lax.fori_loop(0, N_FF_CHUNKS, chunk, 0)
    y = ALPHA * x + 0.5 * (1.0 + gate_ref[0]) * acc_ref[...]
    o_ref[...] = _ln(y, lng_ref[...], lnb_ref[...])


def ffn_block(x, shift, scale, gate, wg, wu, wd, ln_g, ln_b, rows_per_group, ln_in=None):
    n = x.shape[0]
    rows = min(ROWS, n)
    pre_ln = ln_in is not None
    row_spec = pl.BlockSpec((rows, D_MODEL), lambda i: (i, 0))
    vec_spec = _const_spec((1, D_MODEL))
    in_specs = [row_spec] + [_mod_spec(m, rows, rows_per_group) for m in (shift, scale, gate)] + [
        _const_spec(wg.shape), _const_spec(wu.shape), _const_spec(wd.shape), vec_spec, vec_spec]
    args = [x, shift, scale, gate, wg, wu, wd, ln_g.reshape(1, D_MODEL), ln_b.reshape(1, D_MODEL)]
    if pre_ln:
        in_specs += [vec_spec, vec_spec]
        args += [ln_in[0].reshape(1, D_MODEL), ln_in[1].reshape(1, D_MODEL)]
    return pl.pallas_call(
        functools.partial(_ffn_kernel, pre_ln),
        grid=(n // rows,),
        in_specs=in_specs,
        out_specs=row_spec,
        out_shape=jax.ShapeDtypeStruct((n, D_MODEL), jnp.float32),
        scratch_shapes=[pltpu.VMEM((rows, D_MODEL), jnp.float32)],
        compiler_params=_params(),
        name="ffn_block",
    )(*args)


def _ffn_weights(w_in, w_out):
    wg = w_in[:, :D_FF].reshape(D_MODEL, N_FF_CHUNKS, FF_CHUNK).transpose(1, 0, 2).astype(jnp.bfloat16)
    wu = w_in[:, D_FF:].reshape(D_MODEL, N_FF_CHUNKS, FF_CHUNK).transpose(1, 0, 2).astype(jnp.bfloat16)
    wd = w_out.reshape(N_FF_CHUNKS, FF_CHUNK, D_MODEL).astype(jnp.bfloat16)
    return wg, wu, wd


def _mixproj_kernel(x_ref, shift_ref, scale_ref, wp_ref, gq_ref, gkv_ref, wuq_ref, wuk_ref, cos_ref, sin_ref,
                    aq_ref, ak_ref, av_ref, kv_ref, iq_ref, misc_ref, ik_ref, ckv_ref, kpe_ref, kmla_ref, qmla_ref, cu_ref):
    bf16 = jnp.bfloat16
    h = (x_ref[...] * (1.0 + scale_ref[0]) + shift_ref[0]).astype(bf16)
    p = jnp.dot(h, wp_ref[...], preferred_element_type=jnp.float32)
    aq_ref[...] = (p[:, COL_AQ:COL_AK] * (HEAD_DIM ** -0.5)).astype(bf16)
    ak_ref[...] = p[:, COL_AK:COL_AV]
    av_ref[...] = p[:, COL_AV:COL_IQ]
    kv_ref[...] = p[:, COL_AK:COL_IQ].astype(bf16)
    iq_ref[...] = (p[:, COL_IQ:COL_CQ] * (IDX_DIM ** -0.5)).astype(bf16)
    lane = lax.broadcasted_iota(jnp.int32, (1, LANES), 1)
    is_w = (lane >= IDX_DIM) & (lane < IDX_DIM + IDX_HEADS)
    misc = p[:, COL_MISC:N_PROJ] * jnp.where(is_w, IDX_HEADS ** -0.5, 1.0)
    misc_ref[...] = misc
    ik_ref[...] = misc.astype(bf16)
    cos = cos_ref[...]
    sin = sin_ref[...]
    kpe = p[:, COL_KA:COL_KB] * cos + p[:, COL_KB:COL_MISC] * sin
    kpe_ref[...] = kpe
    ckv_in = p[:, COL_CKV:COL_CU]
    ckv = ckv_in * lax.rsqrt(jnp.mean(ckv_in * ckv_in, -1, keepdims=True) + 1e-6) * gkv_ref[...]
    ckv_ref[...] = ckv
    kmla_ref[...] = jnp.concatenate([ckv, kpe], axis=1).astype(bf16)
    cq_in = p[:, COL_CQ:COL_CKV]
    cq = (cq_in * lax.rsqrt(jnp.mean(cq_in * cq_in, -1, keepdims=True) + 1e-6) * gq_ref[...]).astype(bf16)
    q = jnp.dot(cq, wuq_ref[...], preferred_element_type=jnp.float32)
    q_lat = jnp.dot(q[:, UQ_NOPE:UQ_A].astype(bf16), wuk_ref[...], preferred_element_type=jnp.float32)
    pieces = []
    for hd in range(B_HEADS):
        q_pe = q[:, UQ_A + hd * LANES:UQ_A + (hd + 1) * LANES] * cos + q[:, UQ_B + hd * LANES:UQ_B + (hd + 1) * LANES] * sin
        pieces += [q_lat[:, hd * KV_RANK:(hd + 1) * KV_RANK], q_pe]
    qmla_ref[...] = jnp.concatenate(pieces, axis=1).astype(bf16)
    cu_ref[...] = p[:, COL_CU:COL_KA]


def mixproj(x, shift, scale, wp, gq, gkv, wuq, wuk, cos, sin, rows_per_group):
    n = x.shape[0]
    rows = min(ROWS, n)
    f32, bf16 = jnp.float32, jnp.bfloat16

    def rspec(w):
        return pl.BlockSpec((rows, w), lambda i: (i, 0))

    out_w = [(A_HEADS * LANES, bf16), (LANES, f32), (LANES, f32), (2 * LANES, bf16), (IDX_HEADS * LANES, bf16),
             (LANES, f32), (LANES, bf16), (KV_RANK, f32), (LANES, f32), (2 * LANES, bf16),
             (B_HEADS * 2 * LANES, bf16), (C_WIDTH, f32)]
    return pl.pallas_call(
        _mixproj_kernel,
        grid=(n // rows,),
        in_specs=[rspec(D_MODEL), _mod_spec(shift, rows, rows_per_group), _mod_spec(scale, rows, rows_per_group),
                  _const_spec(wp.shape), _const_spec((1, Q_RANK)), _const_spec((1, KV_RANK)),
                  _const_spec(wuq.shape), _const_spec(wuk.shape), rspec(LANES), rspec(LANES)],
        out_specs=[rspec(w) for w, _ in out_w],
        out_shape=[jax.ShapeDtypeStruct((n, w), dt) for w, dt in out_w],
        compiler_params=_params(),
        name="mixproj",
    )(x, shift, scale, wp, gq.reshape(1, Q_RANK), gkv.reshape(1, KV_RANK), wuq, wuk, cos, sin)


def _mix_weights(w_in, w_uq, w_uk):
    f32 = jnp.float32
    off = np.concatenate([[0], np.cumsum(IN_SPLITS)])
    a_q, a_k, a_v, i_q, i_w, i_k, b_cq, b_ckv, b_kpe, c_u = [w_in[:, off[i]:off[i + 1]] for i in range(10)]
    wp = jnp.zeros((D_MODEL, N_PROJ), f32)
    grp = A_HEADS // A_KV_HEADS
    for hd in range(A_HEADS):
        lo = COL_AQ + hd * LANES + HEAD_DIM * (hd // grp)
        wp = wp.at[:, lo:lo + HEAD_DIM].set(a_q[:, hd * HEAD_DIM:(hd + 1) * HEAD_DIM])
    wp = wp.at[:, COL_AK:COL_AK + LANES].set(a_k)
    wp = wp.at[:, COL_AV:COL_AV + LANES].set(a_v)
    for hd in range(IDX_HEADS):
        lo = COL_IQ + hd * LANES
        wp = wp.at[:, lo:lo + IDX_DIM].set(i_q[:, hd * IDX_DIM:(hd + 1) * IDX_DIM])
    wp = wp.at[:, COL_CQ:COL_CQ + Q_RANK].set(b_cq)
    wp = wp.at[:, COL_CKV:COL_CKV + KV_RANK].set(b_ckv)
    wp = wp.at[:, COL_CU:COL_CU + C_WIDTH].set(c_u)
    half = ROPE_DIM // 2
    wp = wp.at[:, COL_KA:COL_KA + ROPE_DIM].set(b_kpe)
    wp = wp.at[:, COL_KB:COL_KB + half].set(-b_kpe[:, half:])
    wp = wp.at[:, COL_KB + half:COL_KB + ROPE_DIM].set(b_kpe[:, :half])
    wp = wp.at[:, COL_MISC:COL_MISC + IDX_DIM].set(i_k)
    wp = wp.at[:, COL_MISC + IDX_DIM:COL_MISC + IDX_DIM + IDX_HEADS].set(i_w)
    wuq = jnp.zeros((Q_RANK, N_UQ), f32)
    wuk = jnp.zeros((B_HEADS * NOPE_DIM, B_HEADS * KV_RANK), f32)
    for hd in range(B_HEADS):
        wuq = wuq.at[:, UQ_NOPE + hd * NOPE_DIM:UQ_NOPE + (hd + 1) * NOPE_DIM].set(w_uq[:, hd, :NOPE_DIM])
        r = w_uq[:, hd, NOPE_DIM:]
        wuq = wuq.at[:, UQ_A + hd * LANES:UQ_A + hd * LANES + ROPE_DIM].set(r)
        wuq = wuq.at[:, UQ_B + hd * LANES:UQ_B + hd * LANES + half].set(-r[:, half:])
        wuq = wuq.at[:, UQ_B + hd * LANES + half:UQ_B + hd * LANES + ROPE_DIM].set(r[:, :half])
        wuk = wuk.at[hd * NOPE_DIM:(hd + 1) * NOPE_DIM, hd * KV_RANK:(hd + 1) * KV_RANK].set(w_uk[:, hd, :].T)
    return wp.astype(jnp.bfloat16), wuq.astype(jnp.bfloat16), wuk.astype(jnp.bfloat16)


def _rope_tables(pos):
    half = ROPE_DIM // 2
    freqs = ROPE_THETA ** (-jnp.arange(half, dtype=jnp.float32) / half)
    ang = pos.astype(jnp.float32)[:, None] * freqs
    z = jnp.zeros((pos.shape[0], LANES - ROPE_DIM), jnp.float32)
    return (jnp.concatenate([jnp.cos(ang), jnp.cos(ang), z], 1), jnp.concatenate([jnp.sin(ang), jnp.sin(ang), z], 1))


def _idx_attn_kernel(topk, iq_ref, misc_ref, ik_ref, aq_ref, kv_ref, bnear_ref, bfar_ref, o_ref, keys_ref):
    f32, bf16, i32 = jnp.float32, jnp.bfloat16, jnp.int32
    qi = pl.program_id(1)
    t0 = qi * TQ
    n_c = (t0 + TQ + TK - 1) // TK
    n_keys = keys_ref.shape[1]
    t_pos = t0 + lax.broadcasted_iota(i32, (TQ, 1), 0)
    misc = misc_ref[0]
    nt = (((1,), (1,)), ((), ()))

    def score_chunk(c, carry):
        base = pl.multiple_of(c * TK, TK)
        kb = ik_ref[0, pl.ds(base, TK), :]
        sc = jnp.zeros((TQ, TK), f32)
        for hd in range(IDX_HEADS):
            s = lax.dot_general(iq_ref[0, :, hd * LANES:(hd + 1) * LANES], kb, nt, preferred_element_type=f32)
            sc = sc + jnp.maximum(s, 0.0) * misc[:, IDX_DIM + hd:IDX_DIM + hd + 1]
        s_pos = base + lax.broadcasted_iota(i32, (TQ, TK), 1)
        keys_ref[:, pl.ds(base, TK)] = _to_key(jnp.where(s_pos <= t_pos, sc, -jnp.inf))
        return carry

    lax.fori_loop(0, n_c, score_chunk, 0)

    def fold(m):
        return m[:, 0:LANES] + m[:, LANES:2 * LANES] + m[:, 2 * LANES:3 * LANES] + m[:, 3 * LANES:4 * LANES]

    def count(pred):
        def body(c, acc):
            base = pl.multiple_of(c * TK, TK)
            k = keys_ref[:, pl.ds(base, TK)]
            idx = base + lax.broadcasted_iota(i32, (TQ, TK), 1)
            return acc + fold(jnp.where(pred(k, idx), 1.0, 0.0))
        acc = lax.fori_loop(0, n_c, body, jnp.zeros((TQ, LANES), f32))
        return jnp.sum(acc, axis=1, keepdims=True)

    kf = float(topk)
    ans0 = jnp.where(count(lambda k, idx: k >= 0) >= kf, 0, INT_MIN).astype(i32)

    def bit_step(i, ans):
        cand = ans | jnp.left_shift(jnp.int32(1), 30 - i)
        return jnp.where(count(lambda k, idx: k >= cand) >= kf, cand, ans)

    tau = lax.fori_loop(0, 31, bit_step, ans0)
    n_gt = count(lambda k, idx: k > tau)
    n_eq = count(lambda k, idx: k == tau)
    need = kf - n_gt

    idx_bits = n_keys.bit_length()

    def tie_search(_):
        def step(i, x):
            cand = x | jnp.left_shift(jnp.int32(1), idx_bits - 1 - i)
            return jnp.where(count(lambda k, idx: (k == tau) & (idx < cand)) < need, cand, x)
        return lax.fori_loop(0, idx_bits, step, jnp.zeros((TQ, 1), i32))

    any_excess = jnp.max(jnp.where(n_eq > need, 1.0, 0.0)) > 0.5
    j_cut = lax.cond(any_excess, tie_search, lambda _: jnp.full((TQ, 1), n_keys, i32), 0)

    q4 = jnp.concatenate([aq_ref[0, :, hd * LANES:(hd + 1) * LANES] for hd in range(A_HEADS)], axis=0)
    near0 = t0 - TQ

    def attend(carry, kv, k, idx, bias, limit_lo, limit_hi):
        m, l, acc = carry
        sel = ((k > tau) | ((k == tau) & (idx <= j_cut))) & (idx >= limit_lo) & (idx < limit_hi) & (idx <= t_pos)
        w = kv.shape[0]
        s = lax.dot_general(q4, kv[:, :LANES], nt, preferred_element_type=f32).reshape(A_HEADS, TQ, w)
        s = jnp.where(sel[None], s + bias, -jnp.inf)
        m_new = jnp.maximum(m, jnp.max(s, axis=-1, keepdims=True))
        m_safe = jnp.where(m_new == -jnp.inf, 0.0, m_new)
        p = jnp.exp(s - m_safe)
        corr = jnp.exp(m - m_safe)
        l = corr * l + jnp.sum(p, axis=-1, keepdims=True)
        pv = jnp.dot(p.reshape(A_HEADS * TQ, w).astype(bf16), kv[:, LANES:], preferred_element_type=f32)
        acc = corr * acc + pv.reshape(A_HEADS, TQ, LANES)
        return m_new, l, acc

    carry0 = (jnp.full((A_HEADS, TQ, 1), -jnp.inf, f32), jnp.zeros((A_HEADS, TQ, 1), f32), jnp.zeros((A_HEADS, TQ, LANES), f32))
    bias_far = bfar_ref[...]

    def far_chunk(c, carry):
        base = pl.multiple_of(c * TK, TK)
        idx = base + lax.broadcasted_iota(i32, (TQ, TK), 1)
        return attend(carry, kv_ref[0, pl.ds(base, TK), :], keys_ref[:, pl.ds(base, TK)], idx, bias_far[:, :, :1], 0, near0)

    carry = lax.fori_loop(0, (jnp.maximum(near0, 0) + TK - 1) // TK, far_chunk, carry0)
    prev = pl.multiple_of(jnp.maximum(near0, 0), TQ)
    diag = pl.multiple_of(t0, TQ)
    kv_near = jnp.concatenate([kv_ref[0, pl.ds(prev, TQ), :], kv_ref[0, pl.ds(diag, TQ), :]], axis=0)
    k_near = jnp.concatenate([keys_ref[:, pl.ds(prev, TQ)], keys_ref[:, pl.ds(diag, TQ)]], axis=1)
    idx_near = near0 + lax.broadcasted_iota(i32, (TQ, 2 * TQ), 1)
    m, l, acc = attend(carry, kv_near, k_near, idx_near, bnear_ref[...], 0, n_keys)
    out = acc / l
    lane = lax.broadcasted_iota(i32, (TQ, LANES), 1)
    blk_a = jnp.where(lane < HEAD_DIM, out[0], pltpu.roll(out[1], HEAD_DIM, 1))
    blk_b = jnp.where(lane < HEAD_DIM, pltpu.roll(out[2], HEAD_DIM, 1), out[3])
    o_ref[0] = jnp.concatenate([blk_a, blk_b], axis=1).astype(o_ref.dtype)


def idx_attn(iq, misc, ik, aq, kv, bias_near, bias_far, topk):
    b_, t_ = iq.shape[0], iq.shape[1]

    def qspec(w):
        return pl.BlockSpec((1, TQ, w), lambda b, i: (b, i, 0))

    def kspec(w):
        return pl.BlockSpec((1, t_, w), lambda b, i: (b, 0, 0))

    return pl.pallas_call(
        functools.partial(_idx_attn_kernel, topk),
        grid=(b_, t_ // TQ),
        in_specs=[qspec(IDX_HEADS * LANES), qspec(LANES), kspec(LANES), qspec(A_HEADS * LANES), kspec(2 * LANES),
                  _const_spec(bias_near.shape), _const_spec(bias_far.shape)],
        out_specs=qspec(A_WIDTH),
        out_shape=jax.ShapeDtypeStruct((b_, t_, A_WIDTH), jnp.bfloat16),
        scratch_shapes=[pltpu.VMEM((TQ, t_), jnp.int32)],
        compiler_params=_params(2),
        name="idx_attn",
    )(iq, misc, ik, aq, kv, bias_near, bias_far)


def _t5_bucket(dist):
    n = jnp.maximum(dist, 0)
    max_exact = REL_BUCKETS // 2
    n_f = jnp.maximum(n, 1).astype(jnp.float32)
    log_bucket = max_exact + (jnp.log(n_f / max_exact) / math.log(REL_MAX_DIST / max_exact) * (REL_BUCKETS - max_exact)).astype(jnp.int32)
    return jnp.where(n < max_exact, n, jnp.minimum(log_bucket, REL_BUCKETS - 1))


def _bias_tables(rel_bias):
    assert TQ >= REL_MAX_DIST
    dist = TQ + jnp.arange(TQ, dtype=jnp.int32)[:, None] - jnp.arange(2 * TQ, dtype=jnp.int32)[None, :]
    near = rel_bias[_t5_bucket(dist)].astype(jnp.float32).transpose(2, 0, 1)
    far = rel_bias[_t5_bucket(jnp.int32(2 * TQ))].astype(jnp.float32)
    return near, jnp.broadcast_to(far[:, None, None], (A_HEADS, 1, LANES))


def _mla_kernel(q_ref, k_ref, wuv_ref, o_ref):
    f32, bf16, i32 = jnp.float32, jnp.bfloat16, jnp.int32
    t0 = pl.program_id(1) * TQ
    w = 2 * LANES
    q4 = jnp.concatenate([q_ref[0, :, hd * w:(hd + 1) * w] for hd in range(B_HEADS)], axis=0)
    t_pos = t0 + lax.broadcasted_iota(i32, (TQ, 1), 0)

    def chunk(c, carry):
        m, l, acc = carry
        base = pl.multiple_of(c * TK, TK)
        kb = k_ref[0, pl.ds(base, TK), :]
        s = lax.dot_general(q4, kb, (((1,), (1,)), ((), ())), preferred_element_type=f32).reshape(B_HEADS, TQ, TK) * MLA_SCALE
        s_pos = base + lax.broadcasted_iota(i32, (TQ, TK), 1)
        s = jnp.where((s_pos <= t_pos)[None], s, -jnp.inf)
        m_new = jnp.maximum(m, jnp.max(s, axis=-1, keepdims=True))
        p = jnp.exp(s - m_new)
        corr = jnp.exp(m - m_new)
        l = corr * l + jnp.sum(p, axis=-1, keepdims=True)
        pv = jnp.dot(p.reshape(B_HEADS * TQ, TK).astype(bf16), kb[:, :KV_RANK], preferred_element_type=f32)
        return m_new, l, corr * acc + pv.reshape(B_HEADS, TQ, KV_RANK)

    carry0 = (jnp.full((B_HEADS, TQ, 1), -jnp.inf, f32), jnp.zeros((B_HEADS, TQ, 1), f32), jnp.zeros((B_HEADS, TQ, KV_RANK), f32))
    m, l, acc = lax.fori_loop(0, (t0 + TQ + TK - 1) // TK, chunk, carry0)
    o_lat = (acc / l).astype(bf16)
    out = jnp.zeros((TQ, B_WIDTH), f32)
    for hd in range(B_HEADS):
        out = out + jnp.dot(o_lat[hd], wuv_ref[hd], preferred_element_type=f32)
    o_ref[0] = out.astype(o_ref.dtype)


def mla_attn(qmla, kmla, wuv):
    b_, t_ = qmla.shape[0], qmla.shape[1]
    return pl.pallas_call(
        _mla_kernel,
        grid=(b_, t_ // TQ),
        in_specs=[pl.BlockSpec((1, TQ, qmla.shape[2]), lambda b, i: (b, i, 0)),
                  pl.BlockSpec((1, t_, 2 * LANES), lambda b, i: (b, 0, 0)),
                  _const_spec(wuv.shape)],
        out_specs=pl.BlockSpec((1, TQ, B_WIDTH), lambda b, i: (b, i, 0)),
        out_shape=jax.ShapeDtypeStruct((b_, t_, B_WIDTH), jnp.bfloat16),
        compiler_params=_params(2),
        name="mla_attn",
    )(qmla, kmla, wuv)


def _wuv_blocks(w_uv):
    out = jnp.zeros((B_HEADS, KV_RANK, B_WIDTH), jnp.float32)
    for hd in range(B_HEADS):
        out = out.at[hd, :, hd * V_DIM:(hd + 1) * V_DIM].set(w_uv[:, hd, :])
    return out.astype(jnp.bfloat16)


def _s5_kernel(n_levels, chunks_per_seq, u_ref, w_ref, wy_ref, apow_ref, y_ref, s_ref):
    f32, i32 = jnp.float32, jnp.int32
    n_rows = u_ref.shape[1]
    cw = S5_CHUNK * C_GROUP
    ye = jnp.dot(u_ref[0], w_ref[0], preferred_element_type=f32)
    e = ye[:, cw:]
    row = lax.broadcasted_iota(i32, (n_rows, 1), 0) % chunks_per_seq
    lane = lax.broadcasted_iota(i32, (1, 2 * C_STATE), 1)
    sign = jnp.where(lane < C_STATE, -1.0, 1.0)
    s = e
    for lv in range(n_levels):
        d = 1 << lv
        a = apow_ref[0, lv:lv + 1, :]
        a_re = jnp.concatenate([a[:, :C_STATE], a[:, :C_STATE]], axis=1)
        a_im = jnp.concatenate([a[:, C_STATE:], a[:, C_STATE:]], axis=1) * sign
        sh = jnp.where(row >= d, pltpu.roll(s, d, 0), 0.0)
        s = s + sh * a_re + pltpu.roll(sh, C_STATE, 1) * a_im
    s_ref[0] = s
    s_prev = jnp.where(row >= 1, pltpu.roll(s, 1, 0), 0.0)
    y_ref[0] = ye[:, :cw] + jnp.dot(s_prev.astype(jnp.bfloat16), wy_ref[0], preferred_element_type=f32)


def s5_prompt(u_g, w_cat, w_y, a_pow, chunks_per_seq):
    g_, n_rows, cw = u_g.shape
    n_levels = a_pow.shape[1]
    return pl.pallas_call(
        functools.partial(_s5_kernel, n_levels, chunks_per_seq),
        grid=(g_,),
        in_specs=[pl.BlockSpec((1, n_rows, cw), lambda g: (g, 0, 0)),
                  pl.BlockSpec((1, cw, cw + 2 * C_STATE), lambda g: (g, 0, 0)),
                  pl.BlockSpec((1, 2 * C_STATE, cw), lambda g: (g, 0, 0)),
                  pl.BlockSpec((1, n_levels, 2 * C_STATE), lambda g: (g, 0, 0))],
        out_specs=[pl.BlockSpec((1, n_rows, cw), lambda g: (g, 0, 0)),
                   pl.BlockSpec((1, n_rows, 2 * C_STATE), lambda g: (g, 0, 0))],
        out_shape=[jax.ShapeDtypeStruct((g_, n_rows, cw), jnp.float32),
                   jax.ShapeDtypeStruct((g_, n_rows, 2 * C_STATE), jnp.float32)],
        compiler_params=_params(),
        name="s5_prompt",
    )(u_g, w_cat, w_y, a_pow)


def _s5_discretise(lam_re, lam_im, log_dt, b_re, b_im, c_re, c_im):
    f32 = jnp.float32
    lam = lax.complex(lam_re.astype(f32), lam_im.astype(f32))
    dt = jnp.exp(log_dt.astype(f32))[:, None]
    a_bar = jnp.exp(lam * dt)
    b_bar = ((a_bar - 1.0) / lam)[..., None] * lax.complex(b_re.astype(f32), b_im.astype(f32))
    c_mat = lax.complex(c_re.astype(f32), c_im.astype(f32))
    return lam * dt, a_bar, b_bar, c_mat


def _s5_weights(lam_re, lam_im, log_dt, b_re, b_im, c_re, c_im, d_skip, n_levels):
    f32 = jnp.float32
    L = S5_CHUNK
    lam_dt, _, b_bar, c_mat = _s5_discretise(lam_re, lam_im, log_dt, b_re, b_im, c_re, c_im)
    taus = jnp.arange(L + 1, dtype=f32)
    a_pw = jnp.exp(lam_dt[None] * taus[:, None, None])
    kern = jnp.einsum('gcp,tgp,gpd->tgcd', c_mat, a_pw[:L], b_bar).real
    t_idx = jnp.arange(L)
    diff = t_idx[None, :] - t_idx[:, None]
    toe = kern[jnp.clip(diff, 0, L - 1)]
    toe = jnp.where((diff >= 0)[:, :, None, None, None], toe, 0.0)
    eye = (diff == 0)[:, :, None, None, None] * (jnp.eye(C_GROUP, dtype=f32)[None, None, None] * d_skip.astype(f32)[None, None, :, :, None])
    toe = (toe + eye).transpose(2, 0, 4, 1, 3).reshape(C_GROUPS, L * C_GROUP, L * C_GROUP)
    e_c = a_pw[L - 1 - t_idx][:, :, :, None] * b_bar[None]
    e_c = e_c.transpose(1, 0, 3, 2).reshape(C_GROUPS, L * C_GROUP, C_STATE)
    w_cat = jnp.concatenate([toe, e_c.real, e_c.imag], axis=2)
    y_c = c_mat[None] * a_pw[1:L + 1][:, :, None, :]
    y_c = y_c.transpose(1, 3, 0, 2).reshape(C_GROUPS, C_STATE, L * C_GROUP)
    w_y = jnp.concatenate([y_c.real, -y_c.imag], axis=1)
    lv = (L * (2.0 ** jnp.arange(n_levels, dtype=f32)))
    a_lv = jnp.exp(lam_dt[:, None, :] * lv[None, :, None])
    a_pow = jnp.concatenate([a_lv.real, a_lv.imag], axis=2)
    return w_cat.astype(jnp.bfloat16), w_y.astype(jnp.bfloat16), a_pow


def _outproj_kernel(x_ref, gate_ref, oa_ref, ob_ref, ys_ref, wglu_ref, bglu_ref, woa_ref, wob_ref, woc_ref, lng_ref, lnb_ref, o_ref):
    f32, bf16 = jnp.float32, jnp.bfloat16
    y = ys_ref[...]
    z = 0.5 * y * (1.0 + jnp.tanh(math.sqrt(2.0 / math.pi) * (y + 0.044715 * (y * y * y))))
    yc = z * jax.nn.sigmoid(jnp.dot(z.astype(bf16), wglu_ref[...], preferred_element_type=f32) + bglu_ref[...])
    mix = jnp.dot(oa_ref[...], woa_ref[...], preferred_element_type=f32)
    mix = mix + jnp.dot(ob_ref[...], wob_ref[...], preferred_element_type=f32)
    mix = mix + jnp.dot(yc.astype(bf16), woc_ref[...], preferred_element_type=f32)
    o_ref[...] = _ln(ALPHA * x_ref[...] + (1.0 + gate_ref[0]) * mix, lng_ref[...], lnb_ref[...])


def outproj(x, gate, oa, ob, ys, wglu, bglu, w_out, ln_g, ln_b, rows_per_group):
    n = x.shape[0]
    rows = min(ROWS, n)
    bf16 = jnp.bfloat16

    def rspec(w):
        return pl.BlockSpec((rows, w), lambda i: (i, 0))

    woa, wob, woc = w_out[:A_WIDTH].astype(bf16), w_out[A_WIDTH:A_WIDTH + B_WIDTH].astype(bf16), w_out[A_WIDTH + B_WIDTH:].astype(bf16)
    vec = _const_spec((1, D_MODEL))
    return pl.pallas_call(
        _outproj_kernel,
        grid=(n // rows,),
        in_specs=[rspec(D_MODEL), _mod_spec(gate, rows, rows_per_group), rspec(A_WIDTH), rspec(B_WIDTH), rspec(C_WIDTH),
                  _const_spec((C_WIDTH, C_WIDTH)), _const_spec((1, C_WIDTH)),
                  _const_spec(woa.shape), _const_spec(wob.shape), _const_spec(woc.shape), vec, vec],
        out_specs=rspec(D_MODEL),
        out_shape=jax.ShapeDtypeStruct((n, D_MODEL), jnp.float32),
        compiler_params=_params(),
        name="outproj",
    )(x, gate, oa, ob, ys, wglu.astype(bf16), bglu.reshape(1, C_WIDTH), woa, wob, woc, ln_g.reshape(1, D_MODEL), ln_b.reshape(1, D_MODEL))


def _prompt_trunk(x, mod, p):
    b_, t_ = x.shape[0], x.shape[1]
    n = b_ * t_
    xf = x.reshape(n, D_MODEL)
    pos = jnp.arange(t_, dtype=jnp.int32)
    cos, sin = _rope_tables(pos)
    cos, sin = jnp.tile(cos, (b_, 1)), jnp.tile(sin, (b_, 1))
    bias_near, bias_far = _bias_tables(p['rel_bias'])
    topk = min(TOPK_MAX, t_ // 4)
    n_chunks = t_ // S5_CHUNK
    n_levels = max(1, (n_chunks - 1).bit_length())
    outs = []
    for l in range(DEPTH):
        m = mod[l].reshape(b_, 3, 3, 1, D_MODEL)

        def mo(j, k):
            return m[:, j, k]

        xf = ffn_block(xf, mo(0, 0), mo(0, 1), mo(0, 2), *p['ffn_w'][l][0], p['ln_g'][l, 0], p['ln_b'][l, 0], t_,
                       ln_in=(p['ln_in_g'], p['ln_in_b']) if l == 0 else None)
        wp, wuq, wuk = p['mix_w'][l]
        aq, ak, av, kv, iq, misc, ik, ckv, kpe, kmla, qmla, cu = mixproj(
            xf, mo(1, 0), mo(1, 1), wp, p['g_qnorm'][l], p['g_kvnorm'][l], wuq, wuk, cos, sin, t_)

        def seq(a):
            return a.reshape(b_, t_, a.shape[-1])

        oa = idx_attn(seq(iq), seq(misc), seq(ik), seq(aq), seq(kv), bias_near, bias_far, topk)
        ob = mla_attn(seq(qmla), seq(kmla), p['wuv'][l])
        w_cat, w_y, a_pow = _s5_weights(*p['ssm'][l], n_levels)
        u_g = cu.astype(jnp.bfloat16).reshape(b_ * n_chunks, S5_CHUNK, C_GROUPS, C_GROUP).transpose(2, 0, 1, 3).reshape(
            C_GROUPS, b_ * n_chunks, S5_CHUNK * C_GROUP)
        y_g, s_g = s5_prompt(u_g, w_cat, w_y, a_pow, n_chunks)
        ys = y_g.reshape(C_GROUPS, b_ * n_chunks, S5_CHUNK, C_GROUP).transpose(1, 2, 0, 3).reshape(n, C_WIDTH)
        s_last = s_g.reshape(C_GROUPS, b_, n_chunks, 2 * C_STATE)[:, :, -1].transpose(1, 0, 2)
        xf = outproj(xf, mo(1, 2), oa.reshape(n, A_WIDTH), ob.reshape(n, B_WIDTH), ys, p['w_glu'][l], p['b_glu'][l], p['w_out'][l],
                     p['ln_g'][l, 1], p['ln_b'][l, 1], t_)
        xf = ffn_block(xf, mo(2, 0), mo(2, 1), mo(2, 2), *p['ffn_w'][l][1], p['ln_g'][l, 2], p['ln_b'][l, 2], t_)
        outs.append((ak.reshape(b_, t_, A_KV_HEADS, HEAD_DIM), av.reshape(b_, t_, A_KV_HEADS, HEAD_DIM),
                     seq(misc)[:, :, :IDX_DIM], seq(ckv), seq(kpe)[:, :, :ROPE_DIM], s_last[:, :, :C_STATE], s_last[:, :, C_STATE:]))
    stacked = [jnp.stack(s, axis=0) for s in zip(*outs)]
    return xf.reshape(b_, t_, D_MODEL), stacked


GROUP_PAGES = 16
GK = GROUP_PAGES * PAGE_SIZE
HROWS = 8


def _sample_attn_kernel(layer, topk, n_pages, pt_ref, iq_ref, iw_ref, aq_ref, qm_ref, ikn_ref, kvn_ref, kmn_ref, bias_ref, wuv_ref,
                        idx_hbm, lat_hbm, rope_hbm, ak_hbm, av_hbm, oa_ref, ob_ref,
                        idx_buf, lat_buf, rope_buf, ak_buf, av_buf, keys_ref, sem_i, sem_a):
    f32, bf16, i32 = jnp.float32, jnp.bfloat16, jnp.int32
    b = pl.program_id(0)
    n_seq = pl.num_programs(0)
    n_groups = n_pages // GROUP_PAGES
    past = n_pages * PAGE_SIZE
    nt = (((1,), (1,)), ((), ()))

    def idx_copies(seq, slot):
        return [pltpu.make_async_copy(idx_hbm.at[layer, pt_ref[seq, j]], idx_buf.at[slot, :, pl.ds(j * PAGE_SIZE, PAGE_SIZE)], sem_i.at[slot])
                for j in range(n_pages)]

    def attn_copies(g, slot):
        out = []
        for j in range(GROUP_PAGES):
            phys = pt_ref[b, g * GROUP_PAGES + j]
            cols = pl.ds(j * PAGE_SIZE, PAGE_SIZE)
            out.append(pltpu.make_async_copy(ak_hbm.at[layer, phys], ak_buf.at[slot, :, cols], sem_a.at[slot]))
            out.append(pltpu.make_async_copy(av_hbm.at[layer, phys], av_buf.at[slot, :, cols], sem_a.at[slot]))
            out.append(pltpu.make_async_copy(rope_hbm.at[layer, phys], rope_buf.at[slot, :, cols], sem_a.at[slot]))
            out.append(pltpu.make_async_copy(lat_hbm.at[layer, phys], lat_buf.at[slot, cols, :], sem_a.at[slot]))
        return out

    islot = b % 2

    @pl.when(b == 0)
    def _():
        for c in idx_copies(0, 0):
            c.start()

    @pl.when(b + 1 < n_seq)
    def _():
        for c in idx_copies(b + 1, 1 - islot):
            c.start()

    for c in idx_copies(b, islot):
        c.wait()

    iq = iq_ref[0]
    iw = iw_ref[0]
    keys_ref[...] = jnp.full(keys_ref.shape, INT_MIN + 1, i32)
    for g in range(n_groups):
        kb = idx_buf[islot, :, g * GK:(g + 1) * GK].astype(bf16)
        s = jnp.dot(iq[:, :IDX_DIM], kb, preferred_element_type=f32)
        keys_ref[g:g + 1, :] = _to_key(jnp.sum(jnp.maximum(s, 0.0) * iw, axis=0, keepdims=True))
    for c in attn_copies(0, 0):
        c.start()
    s_new = jnp.sum(iq.astype(f32) * ikn_ref[0].astype(f32), axis=1, keepdims=True)
    key_new = _to_key(jnp.sum(jnp.maximum(s_new, 0.0) * iw, axis=0, keepdims=True))

    rows = keys_ref.shape[0]
    idx_all = lax.broadcasted_iota(i32, (rows, GK), 0) * GK + lax.broadcasted_iota(i32, (rows, GK), 1)

    def count(pred):
        k = keys_ref[...]
        c = jnp.sum(jnp.where(pred(k, idx_all), 1.0, 0.0), axis=1, keepdims=True)
        return jnp.sum(c, axis=0, keepdims=True) + jnp.where(pred(key_new, jnp.int32(past)), 1.0, 0.0)

    kf = float(topk)
    ans0 = jnp.where(count(lambda k, idx: k >= 0) >= kf, 0, INT_MIN).astype(i32)

    def bit_pair(i, ans):
        hi = jnp.left_shift(jnp.int32(1), 29 - 2 * i)
        lo = jnp.left_shift(jnp.int32(1), 28 - 2 * i)
        c_hi, c_both, c_lo = ans | hi, ans | hi | lo, ans | lo
        n_hi = count(lambda k, idx: k >= c_hi)
        n_both = count(lambda k, idx: k >= c_both)
        n_lo = count(lambda k, idx: k >= c_lo)
        return jnp.where(n_both >= kf, c_both, jnp.where(n_hi >= kf, c_hi, jnp.where(n_lo >= kf, c_lo, ans)))

    cand30 = ans0 | jnp.left_shift(jnp.int32(1), 30)
    ans1 = jnp.where(count(lambda k, idx: k >= cand30) >= kf, cand30, ans0)
    tau = lax.fori_loop(0, 15, bit_pair, ans1)
    need = kf - count(lambda k, idx: k > tau)
    n_eq = count(lambda k, idx: k == tau)
    idx_bits = (past + 1).bit_length()

    def tie_search(_):
        def step(i, x):
            cand = x | jnp.left_shift(jnp.int32(1), idx_bits - 1 - i)
            return jnp.where(count(lambda k, idx: (k == tau) & (idx < cand)) < need, cand, x)
        return lax.fori_loop(0, idx_bits, step, jnp.zeros((1, 1), i32))

    j_cut = lax.cond(jnp.max(jnp.where(n_eq > need, 1.0, 0.0)) > 0.5, tie_search, lambda _: jnp.full((1, 1), past + 1, i32), 0)

    def selected(k, idx):
        return (k > tau) | ((k == tau) & (idx <= j_cut))

    aq = aq_ref[0]
    qm = qm_ref[0]

    def online(carry, s, pv_fn):
        m, l, acc = carry
        m_new = jnp.maximum(m, jnp.max(s, axis=-1, keepdims=True))
        m_safe = jnp.where(m_new == -jnp.inf, 0.0, m_new)
        p = jnp.exp(s - m_safe)
        corr = jnp.exp(m - m_safe)
        return m_new, corr * l + jnp.sum(p, axis=-1, keepdims=True), corr * acc + pv_fn(p.astype(bf16))

    def attend_group(g, carry):
        ca, cm = carry
        slot = g % 2

        @pl.when(g + 1 < n_groups)
        def _():
            for c in attn_copies(g + 1, 1 - slot):
                c.start()

        for c in attn_copies(g, slot):
            c.wait()
        base = pl.multiple_of(g * GK, GK)
        k = keys_ref[pl.ds(g, 1), :]
        idx = base + lax.broadcasted_iota(i32, (1, GK), 1)
        sa = jnp.dot(aq, ak_buf[slot].astype(bf16), preferred_element_type=f32) + bias_ref[:, pl.ds(base, GK)]
        v_t = av_buf[slot].astype(bf16)
        ca = online(ca, jnp.where(selected(k, idx), sa, -jnp.inf), lambda p: lax.dot_general(p, v_t, nt, preferred_element_type=f32))
        lat = lat_buf[slot].astype(bf16)
        sm = lax.dot_general(qm[:, :KV_RANK], lat, nt, preferred_element_type=f32)
        sm = sm + jnp.dot(qm[:, KV_RANK:KV_RANK + ROPE_DIM], rope_buf[slot].astype(bf16), preferred_element_type=f32)
        cm = online(cm, sm * MLA_SCALE, lambda p: jnp.dot(p, lat, preferred_element_type=f32))
        return ca, cm

    def init():
        return (jnp.full((HROWS, 1), -jnp.inf, f32), jnp.zeros((HROWS, 1), f32), jnp.zeros((HROWS, LANES), f32))

    ca, cm = lax.fori_loop(0, n_groups, attend_group, (init(), init()))

    def finish(carry, s_new, v_new):
        m, l, acc = carry
        m_new = jnp.maximum(m, s_new)
        m_safe = jnp.where(m_new == -jnp.inf, 0.0, m_new)
        p = jnp.exp(s_new - m_safe)
        corr = jnp.exp(m - m_safe)
        return (corr * acc + p * v_new) / (corr * l + p)

    kvn = kvn_ref[0].astype(f32)
    sa_new = jnp.sum(aq.astype(f32) * kvn[:, :LANES], axis=1, keepdims=True) + bias_ref[:, past:past + 1]
    sa_new = jnp.where(selected(key_new, jnp.int32(past)), sa_new, -jnp.inf)
    out_a = finish(ca, sa_new, kvn[:, LANES:])
    kmn = kmn_ref[0].astype(f32)
    sm_new = jnp.sum(qm.astype(f32) * kmn, axis=1, keepdims=True) * MLA_SCALE
    out_m = finish(cm, sm_new, kmn[:, :KV_RANK])
    lane = lax.broadcasted_iota(i32, (1, LANES), 1)
    blk_a = jnp.where(lane < HEAD_DIM, out_a[0:1], pltpu.roll(out_a[1:2], HEAD_DIM, 1))
    blk_b = jnp.where(lane < HEAD_DIM, pltpu.roll(out_a[2:3], HEAD_DIM, 1), out_a[3:4])
    oa_ref[0] = jnp.concatenate([blk_a, blk_b], axis=1).astype(oa_ref.dtype)
    o_lat = out_m.astype(bf16)
    ob = jnp.zeros((1, B_WIDTH), f32)
    for hd in range(B_HEADS):
        ob = ob + jnp.dot(o_lat, wuv_ref[hd], preferred_element_type=f32)[hd:hd + 1]
    ob_ref[0] = ob.astype(ob_ref.dtype)


def sample_attn(layer, page_table, iq, iw, aq, qm, ikn, kvn, kmn, bias, wuv, c_idx, c_lat, c_rope, c_ak, c_av, topk):
    n_seq, n_pages = page_table.shape
    assert n_pages % GROUP_PAGES == 0
    n_groups = n_pages // GROUP_PAGES
    rows = -(-n_groups // HROWS) * HROWS
    past = n_pages * PAGE_SIZE
    f32 = jnp.float32

    def sspec(r, w):
        return pl.BlockSpec((1, r, w), lambda b, pt: (b, 0, 0))

    def cspec(shape):
        nd = len(shape)
        return pl.BlockSpec(shape, lambda b, pt: (0,) * nd, pipeline_mode=pl.Buffered(1))

    hbm = pl.BlockSpec(memory_space=pl.ANY)
    return pl.pallas_call(
        functools.partial(_sample_attn_kernel, layer, topk, n_pages),
        grid_spec=pltpu.PrefetchScalarGridSpec(
            num_scalar_prefetch=1,
            grid=(n_seq,),
            in_specs=[sspec(HROWS, LANES), sspec(HROWS, 1), sspec(HROWS, LANES), sspec(HROWS, 2 * LANES),
                      sspec(1, LANES), sspec(1, 2 * LANES), sspec(1, 2 * LANES), cspec(bias.shape), cspec(wuv.shape),
                      hbm, hbm, hbm, hbm, hbm],
            out_specs=[sspec(1, A_WIDTH), sspec(1, B_WIDTH)],
            scratch_shapes=[pltpu.VMEM((2, IDX_DIM, past), f32), pltpu.VMEM((2, GK, KV_RANK), f32), pltpu.VMEM((2, ROPE_DIM, GK), f32),
                            pltpu.VMEM((2, LANES, GK), f32), pltpu.VMEM((2, LANES, GK), f32), pltpu.VMEM((rows, GK), jnp.int32),
                            pltpu.SemaphoreType.DMA((2,)), pltpu.SemaphoreType.DMA((2,))]),
        out_shape=[jax.ShapeDtypeStruct((n_seq, 1, A_WIDTH), jnp.bfloat16), jax.ShapeDtypeStruct((n_seq, 1, B_WIDTH), jnp.bfloat16)],
        compiler_params=_params(),
        name="sample_attn",
    )(page_table, iq, iw, aq, qm, ikn, kvn, kmn, bias, wuv, c_idx, c_lat, c_rope, c_ak, c_av)


def _s5_step_kernel(u_ref, sre_ref, sim_ref, are_ref, aim_ref, bre_ref, bim_ref, cre_ref, cim_ref, d_ref, y_ref, ore_ref, oim_ref):
    hi = lax.Precision.HIGHEST
    u = u_ref[...]
    s_re, s_im, a_re, a_im = sre_ref[...], sim_ref[...], are_ref[...], aim_ref[...]
    n_re = a_re * s_re - a_im * s_im + jnp.dot(u, bre_ref[...], precision=hi, preferred_element_type=jnp.float32)
    n_im = a_re * s_im + a_im * s_re + jnp.dot(u, bim_ref[...], precision=hi, preferred_element_type=jnp.float32)
    ore_ref[...] = n_re
    oim_ref[...] = n_im
    y_ref[...] = (jnp.dot(n_re, cre_ref[...], precision=hi, preferred_element_type=jnp.float32)
                  - jnp.dot(n_im, cim_ref[...], precision=hi, preferred_element_type=jnp.float32) + d_ref[...] * u)


def s5_step(u, s_re, s_im, ops):
    n = u.shape[0]
    n_state = C_GROUPS * C_STATE
    f32 = jnp.float32
    args = (u, s_re, s_im) + tuple(ops)
    return pl.pallas_call(
        _s5_step_kernel,
        grid=(1,),
        in_specs=[_const_spec(a.shape) for a in args],
        out_specs=[pl.BlockSpec((n, C_WIDTH), lambda i: (0, 0)), pl.BlockSpec((n, n_state), lambda i: (0, 0)),
                   pl.BlockSpec((n, n_state), lambda i: (0, 0))],
        out_shape=[jax.ShapeDtypeStruct((n, C_WIDTH), f32), jax.ShapeDtypeStruct((n, n_state), f32), jax.ShapeDtypeStruct((n, n_state), f32)],
        compiler_params=_params(),
        name="s5_step",
    )(*args)


def _s5_step_weights(lam_re, lam_im, log_dt, b_re, b_im, c_re, c_im, d_skip):
    f32 = jnp.float32
    _, a_bar, b_bar, c_mat = _s5_discretise(lam_re, lam_im, log_dt, b_re, b_im, c_re, c_im)
    eye = jnp.eye(C_GROUPS, dtype=f32)

    def bdiag_in(m):
        return jnp.einsum('gpc,gh->gchp', m, eye).reshape(C_WIDTH, C_GROUPS * C_STATE)

    def bdiag_out(m):
        return jnp.einsum('gcp,gh->gphc', m, eye).reshape(C_GROUPS * C_STATE, C_WIDTH)

    return (a_bar.real.reshape(1, -1), a_bar.imag.reshape(1, -1), bdiag_in(b_bar.real), bdiag_in(b_bar.imag),
            bdiag_out(c_mat.real), bdiag_out(c_mat.imag), d_skip.astype(f32).reshape(1, C_WIDTH))


def _sample_trunk(x, mod, p, page_table, caches, state_re, state_im):
    n = x.shape[0]
    n_pages = page_table.shape[1]
    past = n_pages * PAGE_SIZE
    xs = x.reshape(n, D_MODEL)
    cos, sin = _rope_tables(jnp.full((n,), past, jnp.int32))
    dist = past - jnp.arange(past + LANES, dtype=jnp.int32)
    bias = p['rel_bias'][_t5_bucket(dist)].astype(jnp.float32).T
    bias = jnp.concatenate([bias, jnp.zeros((HROWS - A_HEADS, past + LANES), jnp.float32)], axis=0)
    topk = min(TOPK_MAX, (past + 1) // 4)
    outs = []

    def head_rows(a, w):
        a = a.reshape(n, -1, w)
        return jnp.concatenate([a, jnp.zeros((n, HROWS - a.shape[1], w), a.dtype)], axis=1)

    for l in range(DEPTH):
        m = mod[l].reshape(n, 3, 3, D_MODEL)

        def mo(j, k):
            return m[:, j, k].reshape(1, n, D_MODEL)

        xs = ffn_block(xs, mo(0, 0), mo(0, 1), mo(0, 2), *p['ffn_w'][l][0], p['ln_g'][l, 0], p['ln_b'][l, 0], 1,
                       ln_in=(p['ln_in_g'], p['ln_in_b']) if l == 0 else None)
        wp, wuq, wuk = p['mix_w'][l]
        aq, ak, av, kv, iq, misc, ik, ckv, kpe, kmla, qmla, cu = mixproj(
            xs, mo(1, 0), mo(1, 1), wp, p['g_qnorm'][l], p['g_kvnorm'][l], wuq, wuk, cos, sin, 1)
        iw = head_rows(misc[:, IDX_DIM:IDX_DIM + IDX_HEADS], 1)
        oa, ob = sample_attn(l, page_table, head_rows(iq, LANES), iw, head_rows(aq, LANES), head_rows(qmla, 2 * LANES),
                             ik.reshape(n, 1, LANES), kv.reshape(n, 1, 2 * LANES), kmla.reshape(n, 1, 2 * LANES), bias, p['wuv'][l],
                             *caches, topk)
        ys, s_re, s_im = s5_step(cu, state_re[l].reshape(n, -1), state_im[l].reshape(n, -1), _s5_step_weights(*p['ssm'][l]))
        xs = outproj(xs, mo(1, 2), oa.reshape(n, A_WIDTH), ob.reshape(n, B_WIDTH), ys, p['w_glu'][l], p['b_glu'][l], p['w_out'][l],
                     p['ln_g'][l, 1], p['ln_b'][l, 1], 1)
        xs = ffn_block(xs, mo(2, 0), mo(2, 1), mo(2, 2), *p['ffn_w'][l][1], p['ln_g'][l, 2], p['ln_b'][l, 2], 1)
        outs.append((ak.reshape(n, 1, A_KV_HEADS, HEAD_DIM), av.reshape(n, 1, A_KV_HEADS, HEAD_DIM), misc[:, None, :IDX_DIM], ckv[:, None, :],
                     kpe[:, None, :ROPE_DIM], s_re.reshape(n, C_GROUPS, C_STATE), s_im.reshape(n, C_GROUPS, C_STATE)))
    stacked = [jnp.stack(s, axis=0) for s in zip(*outs)]
    return xs.reshape(n, 1, D_MODEL), stacked


def kernel(x_prompt, x_sample, c_prompt, c_sample, page_table, cache_attn_k, cache_attn_v, cache_idx_k, cache_mla_latent, cache_mla_rope, state_ssm_re, state_ssm_im, ln_in_g, ln_in_b, rel_bias, w_ada, b_ada, ln_g, ln_b, w_ffn_in, w_ffn_out, w_in, w_out, g_qnorm, g_kvnorm, w_uq, w_uk, w_uv, ssm_lam_re, ssm_lam_im, ssm_log_dt, ssm_b_re, ssm_b_im, ssm_c_re, ssm_c_im, ssm_d, w_glu, b_glu):
    n_p, n_s = c_prompt.shape[0], c_sample.shape[0]
    n_c = -(-(n_p + n_s) // 8) * 8
    c_all = jnp.concatenate([c_prompt, c_sample, jnp.zeros((n_c - n_p - n_s, D_MODEL), jnp.float32)], axis=0)
    mod_all = adaln(c_all, w_ada, b_ada)
    params = dict(
        ffn_w=[[_ffn_weights(w_ffn_in[l, j], w_ffn_out[l, j]) for j in range(2)] for l in range(DEPTH)],
        mix_w=[_mix_weights(w_in[l], w_uq[l], w_uk[l]) for l in range(DEPTH)],
        wuv=[_wuv_blocks(w_uv[l]) for l in range(DEPTH)],
        ssm=[(ssm_lam_re[l], ssm_lam_im[l], ssm_log_dt[l], ssm_b_re[l], ssm_b_im[l], ssm_c_re[l], ssm_c_im[l], ssm_d[l]) for l in range(DEPTH)],
        ln_in_g=ln_in_g, ln_in_b=ln_in_b, ln_g=ln_g, ln_b=ln_b, rel_bias=rel_bias, g_qnorm=g_qnorm, g_kvnorm=g_kvnorm,
        w_glu=w_glu, b_glu=b_glu, w_out=w_out)
    y_prompt, st_p = _prompt_trunk(x_prompt, mod_all[:, :n_p], params)
    n_pool = cache_attn_k.shape[1]
    caches = (cache_idx_k.transpose(0, 1, 3, 2), cache_mla_latent, cache_mla_rope.transpose(0, 1, 3, 2),
              cache_attn_k.transpose(0, 1, 3, 4, 2).reshape(DEPTH, n_pool, LANES, PAGE_SIZE),
              cache_attn_v.transpose(0, 1, 3, 4, 2).reshape(DEPTH, n_pool, LANES, PAGE_SIZE))
    y_sample, st_s = _sample_trunk(x_sample, mod_all[:, n_p:n_p + n_s], params, page_table, caches, state_ssm_re, state_ssm_im)
    return (y_prompt, y_sample, st_p[0], st_s[0], st_p[1], st_s[1], st_p[2], st_s[2], st_p[3], st_s[3], st_p[4], st_s[4], st_p[5], st_s[5], st_p[6], st_s[6])
```

```python
import functools
import math

import jax
import jax.numpy as jnp
import numpy as np
from jax import lax
from jax.experimental import pallas as pl
from jax.experimental.pallas import tpu as pltpu

D_MODEL = 1024
BATCH = 2
SEQ = 8192
DEPTH = 2
DEC_BATCH = 128
DEC_SEQ = 1
PAST_LEN = 16384
PAGE_SIZE = 128
HEAD_DIM = 64
A_HEADS = 4
A_KV_HEADS = 2
IDX_HEADS = 4
IDX_DIM = 32
TOPK_MAX = 256
REL_BUCKETS = 32
REL_MAX_DIST = 128
B_HEADS = 4
Q_RANK = 256
KV_RANK = 128
NOPE_DIM = 64
ROPE_DIM = 32
V_DIM = 64
ROPE_THETA = 10000.0
A_WIDTH = A_HEADS * HEAD_DIM
B_WIDTH = B_HEADS * V_DIM
C_WIDTH = D_MODEL - A_WIDTH - B_WIDTH
C_GROUP = 16
C_GROUPS = C_WIDTH // C_GROUP
C_STATE = 64
D_FF = 2816
Q_BLOCK = 128
LN_EPS = 1e-5
ALPHA = (2.0 * DEPTH) ** 0.25
MLA_SCALE = (NOPE_DIM + ROPE_DIM) ** -0.5
IN_SPLITS = (A_HEADS * HEAD_DIM, A_KV_HEADS * HEAD_DIM, A_KV_HEADS * HEAD_DIM, IDX_HEADS * IDX_DIM, IDX_HEADS, IDX_DIM, Q_RANK, KV_RANK, ROPE_DIM, C_WIDTH)
D_IN = sum(IN_SPLITS)

VMEM_LIMIT_V7X = 56 * 1024 * 1024
LANES = 128
FF_CHUNK = 256
N_FF_CHUNKS = D_FF // FF_CHUNK
ROWS = 512
TQ = 128
TQ_MLA = 256
TK = 512
S5_CHUNK = 16
INT_MIN = -2 ** 31

COL_AQ = 0
COL_AK = COL_AQ + A_HEADS * LANES
COL_AV = COL_AK + LANES
COL_IQ = COL_AV + LANES
COL_CQ = COL_IQ + IDX_HEADS * LANES
COL_CKV = COL_CQ + Q_RANK
COL_CU = COL_CKV + KV_RANK
COL_KA = COL_CU + C_WIDTH
COL_KB = COL_KA + LANES
COL_MISC = COL_KB + LANES
N_PROJ = COL_MISC + LANES
UQ_NOPE = 0
UQ_A = B_HEADS * NOPE_DIM
UQ_B = UQ_A + B_HEADS * LANES
N_UQ = UQ_B + B_HEADS * LANES


def _params(n_axes=1):
    return pltpu.CompilerParams(dimension_semantics=("arbitrary",) * n_axes, vmem_limit_bytes=VMEM_LIMIT_V7X)


def _ln(v, g, b):
    mu = jnp.mean(v, -1, keepdims=True)
    d = v - mu
    var = jnp.mean(d * d, -1, keepdims=True)
    return d * lax.rsqrt(var + LN_EPS) * g + b


def _mod_spec(mod, rows, rows_per_group):
    if mod.shape[1] == 1:
        return pl.BlockSpec((1, 1, D_MODEL), lambda i: (i // (rows_per_group // rows), 0, 0))
    return pl.BlockSpec((1, rows, D_MODEL), lambda i: (0, i, 0))


def _const_spec(shape):
    nd = len(shape)
    return pl.BlockSpec(shape, lambda *_: (0,) * nd, pipeline_mode=pl.Buffered(1))


def _to_key(score):
    bits = lax.bitcast_convert_type(score + 0.0, jnp.int32)
    return jnp.where(bits < 0, bits ^ 0x7FFFFFFF, bits)


ADA_COLS = 1152


def _adaln_kernel(c_ref, w_ref, b_ref, o_ref):
    c = c_ref[...]
    o_ref[0] = jnp.dot(c * jax.nn.sigmoid(c), w_ref[0], preferred_element_type=jnp.float32) + b_ref[0]


def adaln(c, w_ada, b_ada):
    n = c.shape[0]
    n_out = w_ada.shape[2]
    return pl.pallas_call(
        _adaln_kernel,
        grid=(DEPTH, n_out // ADA_COLS),
        in_specs=[pl.BlockSpec((n, D_MODEL), lambda l, j: (0, 0)),
                  pl.BlockSpec((1, D_MODEL, ADA_COLS), lambda l, j: (l, 0, j)),
                  pl.BlockSpec((1, 1, ADA_COLS), lambda l, j: (l, 0, j))],
        out_specs=pl.BlockSpec((1, n, ADA_COLS), lambda l, j: (l, 0, j)),
        out_shape=jax.ShapeDtypeStruct((DEPTH, n, n_out), jnp.float32),
        compiler_params=_params(2),
        name="adaln",
    )(c, w_ada, b_ada.reshape(DEPTH, 1, n_out))


def _ffn_kernel(pre_ln, x_ref, shift_ref, scale_ref, gate_ref, wg_ref, wu_ref, wd_ref, lng_ref, lnb_ref, *rest):
    if pre_ln:
        ing_ref, inb_ref, o_ref, acc_ref = rest
    else:
        o_ref, acc_ref = rest
    x = x_ref[...]
    if pre_ln:
        x = _ln(x, ing_ref[...], inb_ref[...])
    h = (x * (1.0 + scale_ref[0]) + shift_ref[0]).astype(jnp.bfloat16)
    acc_ref[...] = jnp.zeros_like(acc_ref)

    def chunk(c, carry):
        g = jnp.dot(h, wg_ref[c], preferred_element_type=jnp.float32)
        u = jnp.dot(h, wu_ref[c], preferred_element_type=jnp.float32)
        act = (g * jax.nn.sigmoid(g) * u).astype(jnp.bfloat16)
        acc_ref[...] += jnp.dot(act, wd_ref[c], preferred_element_type=jnp.float32)
        return carry

    lax.fori_loop(0, N_FF_CHUNKS, chunk, 0)
    y = ALPHA * x + 0.5 * (1.0 + gate_ref[0]) * acc_ref[...]
    o_ref[...] = _ln(y, lng_ref[...], lnb_ref[...])


def ffn_block(x, shift, scale, gate, wg, wu, wd, ln_g, ln_b, rows_per_group, ln_in=None):
    n = x.shape[0]
    rows = min(ROWS, n)
    pre_ln = ln_in is not None
    row_spec = pl.BlockSpec((rows, D_MODEL), lambda i: (i, 0))
    vec_spec = _const_spec((1, D_MODEL))
    in_specs = [row_spec] + [_mod_spec(m, rows, rows_per_group) for m in (shift, scale, gate)] + [
        _const_spec(wg.shape), _const_spec(wu.shape), _const_spec(wd.shape), vec_spec, vec_spec]
    args = [x, shift, scale, gate, wg, wu, wd, ln_g.reshape(1, D_MODEL), ln_b.reshape(1, D_MODEL)]
    if pre_ln:
        in_specs += [vec_spec, vec_spec]
        args += [ln_in[0].reshape(1, D_MODEL), ln_in[1].reshape(1, D_MODEL)]
    return pl.pallas_call(
        functools.partial(_ffn_kernel, pre_ln),
        grid=(n // rows,),
        in_specs=in_specs,
        out_specs=row_spec,
        out_shape=jax.ShapeDtypeStruct((n, D_MODEL), jnp.float32),
        scratch_shapes=[pltpu.VMEM((rows, D_MODEL), jnp.float32)],
        compiler_params=_params(),
        name="ffn_block",
    )(*args)


def _ffn_weights(w_in, w_out):
    wg = w_in[:, :D_FF].reshape(D_MODEL, N_FF_CHUNKS, FF_CHUNK).transpose(1, 0, 2).astype(jnp.bfloat16)
    wu = w_in[:, D_FF:].reshape(D_MODEL, N_FF_CHUNKS, FF_CHUNK).transpose(1, 0, 2).astype(jnp.bfloat16)
    wd = w_out.reshape(N_FF_CHUNKS, FF_CHUNK, D_MODEL).astype(jnp.bfloat16)
    return wg, wu, wd


def _mixproj_kernel(x_ref, shift_ref, scale_ref, wp_ref, gq_ref, gkv_ref, wuq_ref, wuk_ref, cos_ref, sin_ref,
                    aq_ref, ak_ref, av_ref, kv_ref, iq_ref, misc_ref, ik_ref, ckv_ref, kpe_ref, kmla_ref, qmla_ref, cu_ref):
    bf16 = jnp.bfloat16
    h = (x_ref[...] * (1.0 + scale_ref[0]) + shift_ref[0]).astype(bf16)
    p = jnp.dot(h, wp_ref[...], preferred_element_type=jnp.float32)
    aq_ref[...] = (p[:, COL_AQ:COL_AK] * (HEAD_DIM ** -0.5)).astype(bf16)
    ak_ref[...] = p[:, COL_AK:COL_AV]
    av_ref[...] = p[:, COL_AV:COL_IQ]
    kv_ref[...] = p[:, COL_AK:COL_IQ].astype(bf16)
    iq_ref[...] = (p[:, COL_IQ:COL_CQ] * (IDX_DIM ** -0.5)).astype(bf16)
    lane = lax.broadcasted_iota(jnp.int32, (1, LANES), 1)
    is_w = (lane >= IDX_DIM) & (lane < IDX_DIM + IDX_HEADS)
    misc = p[:, COL_MISC:N_PROJ] * jnp.where(is_w, IDX_HEADS ** -0.5, 1.0)
    misc_ref[...] = misc
    ik_ref[...] = misc.astype(bf16)
    cos = cos_ref[...]
    sin = sin_ref[...]
    kpe = p[:, COL_KA:COL_KB] * cos + p[:, COL_KB:COL_MISC] * sin
    kpe_ref[...] = kpe
    ckv_in = p[:, COL_CKV:COL_CU]
    ckv = ckv_in * lax.rsqrt(jnp.mean(ckv_in * ckv_in, -1, keepdims=True) + 1e-6) * gkv_ref[...]
    ckv_ref[...] = ckv
    kmla_ref[...] = jnp.concatenate([ckv, kpe], axis=1).astype(bf16)
    cq_in = p[:, COL_CQ:COL_CKV]
    cq = (cq_in * lax.rsqrt(jnp.mean(cq_in * cq_in, -1, keepdims=True) + 1e-6) * gq_ref[...]).astype(bf16)
    q = jnp.dot(cq, wuq_ref[...], preferred_element_type=jnp.float32)
    q_lat = jnp.dot(q[:, UQ_NOPE:UQ_A].astype(bf16), wuk_ref[...], preferred_element_type=jnp.float32)
    pieces = []
    for hd in range(B_HEADS):
        q_pe = q[:, UQ_A + hd * LANES:UQ_A + (hd + 1) * LANES] * cos + q[:, UQ_B + hd * LANES:UQ_B + (hd + 1) * LANES] * sin
        pieces += [q_lat[:, hd * KV_RANK:(hd + 1) * KV_RANK], q_pe]
    qmla_ref[...] = jnp.concatenate(pieces, axis=1).astype(bf16)
    cu_ref[...] = p[:, COL_CU:COL_KA]


def mixproj(x, shift, scale, wp, gq, gkv, wuq, wuk, cos, sin, rows_per_group):
    n = x.shape[0]
    rows = min(ROWS, n)
    f32, bf16 = jnp.float32, jnp.bfloat16

    def rspec(w):
        return pl.BlockSpec((rows, w), lambda i: (i, 0))

    out_w = [(A_HEADS * LANES, bf16), (LANES, f32), (LANES, f32), (2 * LANES, bf16), (IDX_HEADS * LANES, bf16),
             (LANES, f32), (LANES, bf16), (KV_RANK, f32), (LANES, f32), (2 * LANES, bf16),
             (B_HEADS * 2 * LANES, bf16), (C_WIDTH, f32)]
    return pl.pallas_call(
        _mixproj_kernel,
        grid=(n // rows,),
        in_specs=[rspec(D_MODEL), _mod_spec(shift, rows, rows_per_group), _mod_spec(scale, rows, rows_per_group),
                  _const_spec(wp.shape), _const_spec((1, Q_RANK)), _const_spec((1, KV_RANK)),
                  _const_spec(wuq.shape), _const_spec(wuk.shape), rspec(LANES), rspec(LANES)],
        out_specs=[rspec(w) for w, _ in out_w],
        out_shape=[jax.ShapeDtypeStruct((n, w), dt) for w, dt in out_w],
        compiler_params=_params(),
        name="mixproj",
    )(x, shift, scale, wp, gq.reshape(1, Q_RANK), gkv.reshape(1, KV_RANK), wuq, wuk, cos, sin)


def _mix_weights(w_in, w_uq, w_uk):
    f32 = jnp.float32
    off = np.concatenate([[0], np.cumsum(IN_SPLITS)])
    a_q, a_k, a_v, i_q, i_w, i_k, b_cq, b_ckv, b_kpe, c_u = [w_in[:, off[i]:off[i + 1]] for i in range(10)]
    wp = jnp.zeros((D_MODEL, N_PROJ), f32)
    grp = A_HEADS // A_KV_HEADS
    for hd in range(A_HEADS):
        lo = COL_AQ + hd * LANES + HEAD_DIM * (hd // grp)
        wp = wp.at[:, lo:lo + HEAD_DIM].set(a_q[:, hd * HEAD_DIM:(hd + 1) * HEAD_DIM])
    wp = wp.at[:, COL_AK:COL_AK + LANES].set(a_k)
    wp = wp.at[:, COL_AV:COL_AV + LANES].set(a_v)
    for hd in range(IDX_HEADS):
        lo = COL_IQ + hd * LANES
        wp = wp.at[:, lo:lo + IDX_DIM].set(i_q[:, hd * IDX_DIM:(hd + 1) * IDX_DIM])
    wp = wp.at[:, COL_CQ:COL_CQ + Q_RANK].set(b_cq)
    wp = wp.at[:, COL_CKV:COL_CKV + KV_RANK].set(b_ckv)
    wp = wp.at[:, COL_CU:COL_CU + C_WIDTH].set(c_u)
    half = ROPE_DIM // 2
    wp = wp.at[:, COL_KA:COL_KA + ROPE_DIM].set(b_kpe)
    wp = wp.at[:, COL_KB:COL_KB + half].set(-b_kpe[:, half:])
    wp = wp.at[:, COL_KB + half:COL_KB + ROPE_DIM].set(b_kpe[:, :half])
    wp = wp.at[:, COL_MISC:COL_MISC + IDX_DIM].set(i_k)
    wp = wp.at[:, COL_MISC + IDX_DIM:COL_MISC + IDX_DIM + IDX_HEADS].set(i_w)
    wuq = jnp.zeros((Q_RANK, N_UQ), f32)
    wuk = jnp.zeros((B_HEADS * NOPE_DIM, B_HEADS * KV_RANK), f32)
    for hd in range(B_HEADS):
        wuq = wuq.at[:, UQ_NOPE + hd * NOPE_DIM:UQ_NOPE + (hd + 1) * NOPE_DIM].set(w_uq[:, hd, :NOPE_DIM])
        r = w_uq[:, hd, NOPE_DIM:]
        wuq = wuq.at[:, UQ_A + hd * LANES:UQ_A + hd * LANES + ROPE_DIM].set(r)
        wuq = wuq.at[:, UQ_B + hd * LANES:UQ_B + hd * LANES + half].set(-r[:, half:])
        wuq = wuq.at[:, UQ_B + hd * LANES + half:UQ_B + hd * LANES + ROPE_DIM].set(r[:, :half])
        wuk = wuk.at[hd * NOPE_DIM:(hd + 1) * NOPE_DIM, hd * KV_RANK:(hd + 1) * KV_RANK].set(w_uk[:, hd, :].T)
    return wp.astype(jnp.bfloat16), wuq.astype(jnp.bfloat16), wuk.astype(jnp.bfloat16)


def _rope_tables(pos):
    half = ROPE_DIM // 2
    freqs = ROPE_THETA ** (-jnp.arange(half, dtype=jnp.float32) / half)
    ang = pos.astype(jnp.float32)[:, None] * freqs
    z = jnp.zeros((pos.shape[0], LANES - ROPE_DIM), jnp.float32)
    return (jnp.concatenate([jnp.cos(ang), jnp.cos(ang), z], 1), jnp.concatenate([jnp.sin(ang), jnp.sin(ang), z], 1))


def _idx_attn_kernel(topk, iq_ref, misc_ref, ik_ref, aq_ref, kv_ref, bnear_ref, bfar_ref, o_ref, keys_ref):
    f32, bf16, i32 = jnp.float32, jnp.bfloat16, jnp.int32
    qi = pl.program_id(1)
    t0 = qi * TQ
    n_c = (t0 + TQ + TK - 1) // TK
    n_keys = keys_ref.shape[1]
    t_pos = t0 + lax.broadcasted_iota(i32, (TQ, 1), 0)
    misc = misc_ref[0]
    nt = (((1,), (1,)), ((), ()))

    iq4 = jnp.concatenate([iq_ref[0, :, hd * LANES:(hd + 1) * LANES] for hd in range(IDX_HEADS)], axis=0)
    w4 = jnp.stack([misc[:, IDX_DIM + hd:IDX_DIM + hd + 1] for hd in range(IDX_HEADS)], axis=0)

    def score_chunk(c, carry):
        base = pl.multiple_of(c * TK, TK)
        kb = ik_ref[0, pl.ds(base, TK), :]
        s = lax.dot_general(iq4, kb, nt, preferred_element_type=f32).reshape(IDX_HEADS, TQ, TK)
        sc = jnp.sum(jnp.maximum(s, 0.0) * w4, axis=0)
        s_pos = base + lax.broadcasted_iota(i32, (TQ, TK), 1)
        keys_ref[:, pl.ds(base, TK)] = _to_key(jnp.where(s_pos <= t_pos, sc, -jnp.inf))
        return carry

    lax.fori_loop(0, n_c, score_chunk, 0)

    def fold(m):
        return m[:, 0:LANES] + m[:, LANES:2 * LANES] + m[:, 2 * LANES:3 * LANES] + m[:, 3 * LANES:4 * LANES]

    def count(pred):
        def body(c, acc):
            base = pl.multiple_of(c * TK, TK)
            k = keys_ref[:, pl.ds(base, TK)]
            idx = base + lax.broadcasted_iota(i32, (TQ, TK), 1)
            return acc + fold(jnp.where(pred(k, idx), 1.0, 0.0))
        acc = lax.fori_loop(0, n_c, body, jnp.zeros((TQ, LANES), f32))
        return jnp.sum(acc, axis=1, keepdims=True)

    kf = float(topk)
    ans0 = jnp.where(count(lambda k, idx: k >= 0) >= kf, 0, INT_MIN).astype(i32)

    def bit_step(i, ans):
        cand = ans | jnp.left_shift(jnp.int32(1), 30 - i)
        return jnp.where(count(lambda k, idx: k >= cand) >= kf, cand, ans)

    tau = lax.fori_loop(0, 31, bit_step, ans0)
    n_gt = count(lambda k, idx: k > tau)
    n_eq = count(lambda k, idx: k == tau)
    need = kf - n_gt

    idx_bits = n_keys.bit_length()

    def tie_search(_):
        def step(i, x):
            cand = x | jnp.left_shift(jnp.int32(1), idx_bits - 1 - i)
            return jnp.where(count(lambda k, idx: (k == tau) & (idx < cand)) < need, cand, x)
        return lax.fori_loop(0, idx_bits, step, jnp.zeros((TQ, 1), i32))

    any_excess = jnp.max(jnp.where(n_eq > need, 1.0, 0.0)) > 0.5
    j_cut = lax.cond(any_excess, tie_search, lambda _: jnp.full((TQ, 1), n_keys, i32), 0)

    q4 = jnp.concatenate([aq_ref[0, :, hd * LANES:(hd + 1) * LANES] for hd in range(A_HEADS)], axis=0)
    near0 = t0 - TQ

    def attend(carry, kv, k, idx, bias, limit_lo, limit_hi):
        m, l, acc = carry
        sel = ((k > tau) | ((k == tau) & (idx <= j_cut))) & (idx >= limit_lo) & (idx < limit_hi) & (idx <= t_pos)
        w = kv.shape[0]
        s = lax.dot_general(q4, kv[:, :LANES], nt, preferred_element_type=f32).reshape(A_HEADS, TQ, w)
        s = jnp.where(sel[None], s + bias, -jnp.inf)
        m_new = jnp.maximum(m, jnp.max(s, axis=-1, keepdims=True))
        m_safe = jnp.where(m_new == -jnp.inf, 0.0, m_new)
        p = jnp.exp(s - m_safe)
        corr = jnp.exp(m - m_safe)
        l = corr * l + jnp.sum(p, axis=-1, keepdims=True)
        pv = jnp.dot(p.reshape(A_HEADS * TQ, w).astype(bf16), kv[:, LANES:], preferred_element_type=f32)
        acc = corr * acc + pv.reshape(A_HEADS, TQ, LANES)
        return m_new, l, acc

    carry0 = (jnp.full((A_HEADS, TQ, 1), -jnp.inf, f32), jnp.zeros((A_HEADS, TQ, 1), f32), jnp.zeros((A_HEADS, TQ, LANES), f32))
    bias_far = bfar_ref[...]

    def far_chunk(c, carry):
        base = pl.multiple_of(c * TK, TK)
        idx = base + lax.broadcasted_iota(i32, (TQ, TK), 1)
        return attend(carry, kv_ref[0, pl.ds(base, TK), :], keys_ref[:, pl.ds(base, TK)], idx, bias_far[:, :, :1], 0, near0)

    carry = lax.fori_loop(0, (jnp.maximum(near0, 0) + TK - 1) // TK, far_chunk, carry0)
    prev = pl.multiple_of(jnp.maximum(near0, 0), TQ)
    diag = pl.multiple_of(t0, TQ)
    kv_near = jnp.concatenate([kv_ref[0, pl.ds(prev, TQ), :], kv_ref[0, pl.ds(diag, TQ), :]], axis=0)
    k_near = jnp.concatenate([keys_ref[:, pl.ds(prev, TQ)], keys_ref[:, pl.ds(diag, TQ)]], axis=1)
    idx_near = near0 + lax.broadcasted_iota(i32, (TQ, 2 * TQ), 1)
    m, l, acc = attend(carry, kv_near, k_near, idx_near, bnear_ref[...], 0, n_keys)
    out = acc / l
    lane = lax.broadcasted_iota(i32, (TQ, LANES), 1)
    blk_a = jnp.where(lane < HEAD_DIM, out[0], pltpu.roll(out[1], HEAD_DIM, 1))
    blk_b = jnp.where(lane < HEAD_DIM, pltpu.roll(out[2], HEAD_DIM, 1), out[3])
    o_ref[0] = jnp.concatenate([blk_a, blk_b], axis=1).astype(o_ref.dtype)


def idx_attn(iq, misc, ik, aq, kv, bias_near, bias_far, topk):
    b_, t_ = iq.shape[0], iq.shape[1]

    def qspec(w):
        return pl.BlockSpec((1, TQ, w), lambda b, i: (b, i, 0))

    def kspec(w):
        return pl.BlockSpec((1, t_, w), lambda b, i: (b, 0, 0))

    return pl.pallas_call(
        functools.partial(_idx_attn_kernel, topk),
        grid=(b_, t_ // TQ),
        in_specs=[qspec(IDX_HEADS * LANES), qspec(LANES), kspec(LANES), qspec(A_HEADS * LANES), kspec(2 * LANES),
                  _const_spec(bias_near.shape), _const_spec(bias_far.shape)],
        out_specs=qspec(A_WIDTH),
        out_shape=jax.ShapeDtypeStruct((b_, t_, A_WIDTH), jnp.bfloat16),
        scratch_shapes=[pltpu.VMEM((TQ, t_), jnp.int32)],
        compiler_params=_params(2),
        name="idx_attn",
    )(iq, misc, ik, aq, kv, bias_near, bias_far)


def _t5_bucket(dist):
    n = jnp.maximum(dist, 0)
    max_exact = REL_BUCKETS // 2
    n_f = jnp.maximum(n, 1).astype(jnp.float32)
    log_bucket = max_exact + (jnp.log(n_f / max_exact) / math.log(REL_MAX_DIST / max_exact) * (REL_BUCKETS - max_exact)).astype(jnp.int32)
    return jnp.where(n < max_exact, n, jnp.minimum(log_bucket, REL_BUCKETS - 1))


def _bias_tables(rel_bias):
    assert TQ >= REL_MAX_DIST
    dist = TQ + jnp.arange(TQ, dtype=jnp.int32)[:, None] - jnp.arange(2 * TQ, dtype=jnp.int32)[None, :]
    near = rel_bias[_t5_bucket(dist)].astype(jnp.float32).transpose(2, 0, 1)
    far = rel_bias[_t5_bucket(jnp.int32(2 * TQ))].astype(jnp.float32)
    return near, jnp.broadcast_to(far[:, None, None], (A_HEADS, 1, LANES))


def _mla_kernel(q_ref, k_ref, wuv_ref, o_ref):
    f32, bf16, i32 = jnp.float32, jnp.bfloat16, jnp.int32
    TQ = TQ_MLA
    t0 = pl.program_id(1) * TQ
    w = 2 * LANES
    q4 = jnp.concatenate([q_ref[0, :, hd * w:(hd + 1) * w] for hd in range(B_HEADS)], axis=0)
    t_pos = t0 + lax.broadcasted_iota(i32, (TQ, 1), 0)

    def chunk(c, carry):
        m, l, acc = carry
        base = pl.multiple_of(c * TK, TK)
        kb = k_ref[0, pl.ds(base, TK), :]
        s = lax.dot_general(q4, kb, (((1,), (1,)), ((), ())), preferred_element_type=f32).reshape(B_HEADS, TQ, TK) * MLA_SCALE
        s_pos = base + lax.broadcasted_iota(i32, (TQ, TK), 1)
        s = jnp.where((s_pos <= t_pos)[None], s, -jnp.inf)
        m_new = jnp.maximum(m, jnp.max(s, axis=-1, keepdims=True))
        p = jnp.exp(s - m_new)
        corr = jnp.exp(m - m_new)
        l = corr * l + jnp.sum(p, axis=-1, keepdims=True)
        pv = jnp.dot(p.reshape(B_HEADS * TQ, TK).astype(bf16), kb[:, :KV_RANK], preferred_element_type=f32)
        return m_new, l, corr * acc + pv.reshape(B_HEADS, TQ, KV_RANK)

    carry0 = (jnp.full((B_HEADS, TQ, 1), -jnp.inf, f32), jnp.zeros((B_HEADS, TQ, 1), f32), jnp.zeros((B_HEADS, TQ, KV_RANK), f32))
    m, l, acc = lax.fori_loop(0, (t0 + TQ + TK - 1) // TK, chunk, carry0)
    o_lat = (acc / l).astype(bf16)
    out = jnp.zeros((TQ, B_WIDTH), f32)
    for hd in range(B_HEADS):
        out = out + jnp.dot(o_lat[hd], wuv_ref[hd], preferred_element_type=f32)
    o_ref[0] = out.astype(o_ref.dtype)


def mla_attn(qmla, kmla, wuv):
    b_, t_ = qmla.shape[0], qmla.shape[1]
    TQ = TQ_MLA
    return pl.pallas_call(
        _mla_kernel,
        grid=(b_, t_ // TQ),
        in_specs=[pl.BlockSpec((1, TQ, qmla.shape[2]), lambda b, i: (b, i, 0)),
                  pl.BlockSpec((1, t_, 2 * LANES), lambda b, i: (b, 0, 0)),
                  _const_spec(wuv.shape)],
        out_specs=pl.BlockSpec((1, TQ, B_WIDTH), lambda b, i: (b, i, 0)),
        out_shape=jax.ShapeDtypeStruct((b_, t_, B_WIDTH), jnp.bfloat16),
        compiler_params=_params(2),
        name="mla_attn",
    )(qmla, kmla, wuv)


def _wuv_blocks(w_uv):
    out = jnp.zeros((B_HEADS, KV_RANK, B_WIDTH), jnp.float32)
    for hd in range(B_HEADS):
        out = out.at[hd, :, hd * V_DIM:(hd + 1) * V_DIM].set(w_uv[:, hd, :])
    return out.astype(jnp.bfloat16)


def _s5_kernel(n_levels, chunks_per_seq, u_ref, w_ref, wy_ref, apow_ref, y_ref, s_ref):
    f32, i32 = jnp.float32, jnp.int32
    n_rows = u_ref.shape[1]
    cw = S5_CHUNK * C_GROUP
    ye = jnp.dot(u_ref[0], w_ref[0], preferred_element_type=f32)
    e = ye[:, cw:]
    row = lax.broadcasted_iota(i32, (n_rows, 1), 0) % chunks_per_seq
    lane = lax.broadcasted_iota(i32, (1, 2 * C_STATE), 1)
    sign = jnp.where(lane < C_STATE, -1.0, 1.0)
    s = e
    for lv in range(n_levels):
        d = 1 << lv
        a = apow_ref[0, lv:lv + 1, :]
        a_re = jnp.concatenate([a[:, :C_STATE], a[:, :C_STATE]], axis=1)
        a_im = jnp.concatenate([a[:, C_STATE:], a[:, C_STATE:]], axis=1) * sign
        sh = jnp.where(row >= d, pltpu.roll(s, d, 0), 0.0)
        s = s + sh * a_re + pltpu.roll(sh, C_STATE, 1) * a_im
    s_ref[0] = s
    s_prev = jnp.where(row >= 1, pltpu.roll(s, 1, 0), 0.0)
    y_ref[0] = ye[:, :cw] + jnp.dot(s_prev.astype(jnp.bfloat16), wy_ref[0], preferred_element_type=f32)


def s5_prompt(u_g, w_cat, w_y, a_pow, chunks_per_seq):
    g_, n_rows, cw = u_g.shape
    n_levels = a_pow.shape[1]
    return pl.pallas_call(
        functools.partial(_s5_kernel, n_levels, chunks_per_seq),
        grid=(g_,),
        in_specs=[pl.BlockSpec((1, n_rows, cw), lambda g: (g, 0, 0)),
                  pl.BlockSpec((1, cw, cw + 2 * C_STATE), lambda g: (g, 0, 0)),
                  pl.BlockSpec((1, 2 * C_STATE, cw), lambda g: (g, 0, 0)),
                  pl.BlockSpec((1, n_levels, 2 * C_STATE), lambda g: (g, 0, 0))],
        out_specs=[pl.BlockSpec((1, n_rows, cw), lambda g: (g, 0, 0)),
                   pl.BlockSpec((1, n_rows, 2 * C_STATE), lambda g: (g, 0, 0))],
        out_shape=[jax.ShapeDtypeStruct((g_, n_rows, cw), jnp.float32),
                   jax.ShapeDtypeStruct((g_, n_rows, 2 * C_STATE), jnp.float32)],
        compiler_params=_params(),
        name="s5_prompt",
    )(u_g, w_cat, w_y, a_pow)


def _s5_discretise(lam_re, lam_im, log_dt, b_re, b_im, c_re, c_im):
    f32 = jnp.float32
    lam = lax.complex(lam_re.astype(f32), lam_im.astype(f32))
    dt = jnp.exp(log_dt.astype(f32))[:, None]
    a_bar = jnp.exp(lam * dt)
    b_bar = ((a_bar - 1.0) / lam)[..., None] * lax.complex(b_re.astype(f32), b_im.astype(f32))
    c_mat = lax.complex(c_re.astype(f32), c_im.astype(f32))
    return lam * dt, a_bar, b_bar, c_mat


def _s5_weights(lam_re, lam_im, log_dt, b_re, b_im, c_re, c_im, d_skip, n_levels):
    f32 = jnp.float32
    L = S5_CHUNK
    lam_dt, _, b_bar, c_mat = _s5_discretise(lam_re, lam_im, log_dt, b_re, b_im, c_re, c_im)
    taus = jnp.arange(L + 1, dtype=f32)
    a_pw = jnp.exp(lam_dt[None] * taus[:, None, None])
    kern = jnp.einsum('gcp,tgp,gpd->tgcd', c_mat, a_pw[:L], b_bar).real
    t_idx = jnp.arange(L)
    diff = t_idx[None, :] - t_idx[:, None]
    toe = kern[jnp.clip(diff, 0, L - 1)]
    toe = jnp.where((diff >= 0)[:, :, None, None, None], toe, 0.0)
    eye = (diff == 0)[:, :, None, None, None] * (jnp.eye(C_GROUP, dtype=f32)[None, None, None] * d_skip.astype(f32)[None, None, :, :, None])
    toe = (toe + eye).transpose(2, 0, 4, 1, 3).reshape(C_GROUPS, L * C_GROUP, L * C_GROUP)
    e_c = a_pw[L - 1 - t_idx][:, :, :, None] * b_bar[None]
    e_c = e_c.transpose(1, 0, 3, 2).reshape(C_GROUPS, L * C_GROUP, C_STATE)
    w_cat = jnp.concatenate([toe, e_c.real, e_c.imag], axis=2)
    y_c = c_mat[None] * a_pw[1:L + 1][:, :, None, :]
    y_c = y_c.transpose(1, 3, 0, 2).reshape(C_GROUPS, C_STATE, L * C_GROUP)
    w_y = jnp.concatenate([y_c.real, -y_c.imag], axis=1)
    lv = (L * (2.0 ** jnp.arange(n_levels, dtype=f32)))
    a_lv = jnp.exp(lam_dt[:, None, :] * lv[None, :, None])
    a_pow = jnp.concatenate([a_lv.real, a_lv.imag], axis=2)
    return w_cat.astype(jnp.bfloat16), w_y.astype(jnp.bfloat16), a_pow


def _outproj_kernel(x_ref, gate_ref, oa_ref, ob_ref, ys_ref, wglu_ref, bglu_ref, woa_ref, wob_ref, woc_ref, lng_ref, lnb_ref, o_ref):
    f32, bf16 = jnp.float32, jnp.bfloat16
    y = ys_ref[...]
    z = 0.5 * y * (1.0 + jnp.tanh(math.sqrt(2.0 / math.pi) * (y + 0.044715 * (y * y * y))))
    yc = z * jax.nn.sigmoid(jnp.dot(z.astype(bf16), wglu_ref[...], preferred_element_type=f32) + bglu_ref[...])
    mix = jnp.dot(oa_ref[...], woa_ref[...], preferred_element_type=f32)
    mix = mix + jnp.dot(ob_ref[...], wob_ref[...], preferred_element_type=f32)
    mix = mix + jnp.dot(yc.astype(bf16), woc_ref[...], preferred_element_type=f32)
    o_ref[...] = _ln(ALPHA * x_ref[...] + (1.0 + gate_ref[0]) * mix, lng_ref[...], lnb_ref[...])


def outproj(x, gate, oa, ob, ys, wglu, bglu, w_out, ln_g, ln_b, rows_per_group):
    n = x.shape[0]
    rows = min(ROWS, n)
    bf16 = jnp.bfloat16

    def rspec(w):
        return pl.BlockSpec((rows, w), lambda i: (i, 0))

    woa, wob, woc = w_out[:A_WIDTH].astype(bf16), w_out[A_WIDTH:A_WIDTH + B_WIDTH].astype(bf16), w_out[A_WIDTH + B_WIDTH:].astype(bf16)
    vec = _const_spec((1, D_MODEL))
    return pl.pallas_call(
        _outproj_kernel,
        grid=(n // rows,),
        in_specs=[rspec(D_MODEL), _mod_spec(gate, rows, rows_per_group), rspec(A_WIDTH), rspec(B_WIDTH), rspec(C_WIDTH),
                  _const_spec((C_WIDTH, C_WIDTH)), _const_spec((1, C_WIDTH)),
                  _const_spec(woa.shape), _const_spec(wob.shape), _const_spec(woc.shape), vec, vec],
        out_specs=rspec(D_MODEL),
        out_shape=jax.ShapeDtypeStruct((n, D_MODEL), jnp.float32),
        compiler_params=_params(),
        name="outproj",
    )(x, gate, oa, ob, ys, wglu.astype(bf16), bglu.reshape(1, C_WIDTH), woa, wob, woc, ln_g.reshape(1, D_MODEL), ln_b.reshape(1, D_MODEL))


def _prompt_trunk(x, mod, p):
    b_, t_ = x.shape[0], x.shape[1]
    n = b_ * t_
    xf = x.reshape(n, D_MODEL)
    pos = jnp.arange(t_, dtype=jnp.int32)
    cos, sin = _rope_tables(pos)
    cos, sin = jnp.tile(cos, (b_, 1)), jnp.tile(sin, (b_, 1))
    bias_near, bias_far = _bias_tables(p['rel_bias'])
    topk = min(TOPK_MAX, t_ // 4)
    n_chunks = t_ // S5_CHUNK
    n_levels = max(1, (n_chunks - 1).bit_length())
    outs = []
    for l in range(DEPTH):
        m = mod[l].reshape(b_, 3, 3, 1, D_MODEL)

        def mo(j, k):
            return m[:, j, k]

        xf = ffn_block(xf, mo(0, 0), mo(0, 1), mo(0, 2), *p['ffn_w'][l][0], p['ln_g'][l, 0], p['ln_b'][l, 0], t_,
                       ln_in=(p['ln_in_g'], p['ln_in_b']) if l == 0 else None)
        wp, wuq, wuk = p['mix_w'][l]
        aq, ak, av, kv, iq, misc, ik, ckv, kpe, kmla, qmla, cu = mixproj(
            xf, mo(1, 0), mo(1, 1), wp, p['g_qnorm'][l], p['g_kvnorm'][l], wuq, wuk, cos, sin, t_)

        def seq(a):
            return a.reshape(b_, t_, a.shape[-1])

        oa = idx_attn(seq(iq), seq(misc), seq(ik), seq(aq), seq(kv), bias_near, bias_far, topk)
        ob = mla_attn(seq(qmla), seq(kmla), p['wuv'][l])
        w_cat, w_y, a_pow = _s5_weights(*p['ssm'][l], n_levels)
        u_g = cu.astype(jnp.bfloat16).reshape(b_ * n_chunks, S5_CHUNK, C_GROUPS, C_GROUP).transpose(2, 0, 1, 3).reshape(
            C_GROUPS, b_ * n_chunks, S5_CHUNK * C_GROUP)
        y_g, s_g = s5_prompt(u_g, w_cat, w_y, a_pow, n_chunks)
        ys = y_g.reshape(C_GROUPS, b_ * n_chunks, S5_CHUNK, C_GROUP).transpose(1, 2, 0, 3).reshape(n, C_WIDTH)
        s_last = s_g.reshape(C_GROUPS, b_, n_chunks, 2 * C_STATE)[:, :, -1].transpose(1, 0, 2)
        xf = outproj(xf, mo(1, 2), oa.reshape(n, A_WIDTH), ob.reshape(n, B_WIDTH), ys, p['w_glu'][l], p['b_glu'][l], p['w_out'][l],
                     p['ln_g'][l, 1], p['ln_b'][l, 1], t_)
        xf = ffn_block(xf, mo(2, 0), mo(2, 1), mo(2, 2), *p['ffn_w'][l][1], p['ln_g'][l, 2], p['ln_b'][l, 2], t_)
        outs.append((ak.reshape(b_, t_, A_KV_HEADS, HEAD_DIM), av.reshape(b_, t_, A_KV_HEADS, HEAD_DIM),
                     seq(misc)[:, :, :IDX_DIM], seq(ckv), seq(kpe)[:, :, :ROPE_DIM], s_last[:, :, :C_STATE], s_last[:, :, C_STATE:]))
    stacked = [jnp.stack(s, axis=0) for s in zip(*outs)]
    return xf.reshape(b_, t_, D_MODEL), stacked


GROUP_PAGES = 16
GK = GROUP_PAGES * PAGE_SIZE
HROWS = 8
ATTN_AHEAD = 2
ATTN_SLOTS = ATTN_AHEAD + 1


def _sample_attn_kernel(layer, topk, n_pages, pt_ref, iq_ref, iw_ref, aq_ref, qm_ref, ikn_ref, kvn_ref, kmn_ref, bias_ref, wuv_ref,
                        idx_hbm, lat_hbm, rope_hbm, ak_hbm, av_hbm, oa_ref, ob_ref,
                        idx_buf, lat_buf, rope_buf, ak_buf, av_buf, keys_ref, sem_i, sem_a):
    f32, bf16, i32 = jnp.float32, jnp.bfloat16, jnp.int32
    b = pl.program_id(0)
    n_seq = pl.num_programs(0)
    n_groups = n_pages // GROUP_PAGES
    past = n_pages * PAGE_SIZE
    nt = (((1,), (1,)), ((), ()))

    def idx_copies(seq, slot):
        return [pltpu.make_async_copy(idx_hbm.at[layer, pt_ref[seq, j]], idx_buf.at[slot, :, pl.ds(j * PAGE_SIZE, PAGE_SIZE)], sem_i.at[slot])
                for j in range(n_pages)]

    def attn_copies(seq, g, slot):
        out = []
        for j in range(GROUP_PAGES):
            phys = pt_ref[seq, g * GROUP_PAGES + j]
            cols = pl.ds(j * PAGE_SIZE, PAGE_SIZE)
            out.append(pltpu.make_async_copy(ak_hbm.at[layer, phys], ak_buf.at[slot, :, cols], sem_a.at[slot]))
            out.append(pltpu.make_async_copy(av_hbm.at[layer, phys], av_buf.at[slot, :, cols], sem_a.at[slot]))
            out.append(pltpu.make_async_copy(rope_hbm.at[layer, phys], rope_buf.at[slot, :, cols], sem_a.at[slot]))
            out.append(pltpu.make_async_copy(lat_hbm.at[layer, phys], lat_buf.at[slot, cols, :], sem_a.at[slot]))
        return out

    islot = b % 2

    @pl.when(b == 0)
    def _():
        for c in idx_copies(0, 0):
            c.start()
        for g in range(ATTN_AHEAD):
            for c in attn_copies(0, g, g):
                c.start()

    @pl.when(b + 1 < n_seq)
    def _():
        for c in idx_copies(b + 1, 1 - islot):
            c.start()

    for c in idx_copies(b, islot):
        c.wait()

    iq = iq_ref[0]
    iw = iw_ref[0]
    keys_ref[...] = jnp.full(keys_ref.shape, INT_MIN + 1, i32)
    for g in range(n_groups):
        kb = idx_buf[islot, :, g * GK:(g + 1) * GK].astype(bf16)
        s = jnp.dot(iq[:, :IDX_DIM], kb, preferred_element_type=f32)
        keys_ref[g:g + 1, :] = _to_key(jnp.sum(jnp.maximum(s, 0.0) * iw, axis=0, keepdims=True))
    s_new =jnp.sum(iq.astype(f32) * ikn_ref[0].astype(f32), axis=1, keepdims=True)
    key_new = _to_key(jnp.sum(jnp.maximum(s_new, 0.0) * iw, axis=0, keepdims=True))

    rows = keys_ref.shape[0]
    idx_all = lax.broadcasted_iota(i32, (rows, GK), 0) * GK + lax.broadcasted_iota(i32, (rows, GK), 1)

    def count(pred):
        k = keys_ref[...]
        c = jnp.sum(jnp.where(pred(k, idx_all), 1.0, 0.0), axis=1, keepdims=True)
        return jnp.sum(c, axis=0, keepdims=True) + jnp.where(pred(key_new, jnp.int32(past)), 1.0, 0.0)

    kf = float(topk)
    ans0 = jnp.where(count(lambda k, idx: k >= 0) >= kf, 0, INT_MIN).astype(i32)

    def bit_pair(i, ans):
        hi = jnp.left_shift(jnp.int32(1), 29 - 2 * i)
        lo = jnp.left_shift(jnp.int32(1), 28 - 2 * i)
        c_hi, c_both, c_lo = ans | hi, ans | hi | lo, ans | lo
        n_hi = count(lambda k, idx: k >= c_hi)
        n_both = count(lambda k, idx: k >= c_both)
        n_lo = count(lambda k, idx: k >= c_lo)
        return jnp.where(n_both >= kf, c_both, jnp.where(n_hi >= kf, c_hi, jnp.where(n_lo >= kf, c_lo, ans)))

    cand30 = ans0 | jnp.left_shift(jnp.int32(1), 30)
    ans1 = jnp.where(count(lambda k, idx: k >= cand30) >= kf, cand30, ans0)
    tau = lax.fori_loop(0, 15, bit_pair, ans1)
    need = kf - count(lambda k, idx: k > tau)
    n_eq = count(lambda k, idx: k == tau)
    idx_bits = (past + 1).bit_length()

    def tie_search(_):
        def step(i, x):
            cand = x | jnp.left_shift(jnp.int32(1), idx_bits - 1 - i)
            return jnp.where(count(lambda k, idx: (k == tau) & (idx < cand)) < need, cand, x)
        return lax.fori_loop(0, idx_bits, step, jnp.zeros((1, 1), i32))

    j_cut = lax.cond(jnp.max(jnp.where(n_eq > need, 1.0, 0.0)) > 0.5, tie_search, lambda _: jnp.full((1, 1), past + 1, i32), 0)

    def selected(k, idx):
        return (k > tau) | ((k == tau) & (idx <= j_cut))

    aq = aq_ref[0]
    qm = qm_ref[0]

    def online(carry, s, pv_fn):
        m, l, acc = carry
        m_new = jnp.maximum(m, jnp.max(s, axis=-1, keepdims=True))
        m_safe = jnp.where(m_new == -jnp.inf, 0.0, m_new)
        p = jnp.exp(s - m_safe)
        corr = jnp.exp(m - m_safe)
        return m_new, corr * l + jnp.sum(p, axis=-1, keepdims=True), corr * acc + pv_fn(p.astype(bf16))

    def attend_group(g, carry):
        ca, cm = carry
        gg = b * n_groups + g
        slot = gg % ATTN_SLOTS
        ahead = gg + ATTN_AHEAD
        seq_a = ahead // n_groups

        @pl.when(seq_a < n_seq)
        def _():
            for c in attn_copies(seq_a, ahead % n_groups, ahead % ATTN_SLOTS):
                c.start()

        for c in attn_copies(b, g, slot):
            c.wait()
        base = pl.multiple_of(g * GK, GK)
        k = keys_ref[pl.ds(g, 1), :]
        idx = base + lax.broadcasted_iota(i32, (1, GK), 1)
        sa = jnp.dot(aq, ak_buf[slot].astype(bf16), preferred_element_type=f32) + bias_ref[:, pl.ds(base, GK)]
        v_t = av_buf[slot].astype(bf16)
        ca = online(ca, jnp.where(selected(k, idx), sa, -jnp.inf), lambda p: lax.dot_general(p, v_t, nt, preferred_element_type=f32))
        lat = lat_buf[slot].astype(bf16)
        sm = lax.dot_general(qm[:, :KV_RANK], lat, nt, preferred_element_type=f32)
        sm = sm + jnp.dot(qm[:, KV_RANK:KV_RANK + ROPE_DIM], rope_buf[slot].astype(bf16), preferred_element_type=f32)
        cm = online(cm, sm * MLA_SCALE, lambda p: jnp.dot(p, lat, preferred_element_type=f32))
        return ca, cm

    def init():
        return (jnp.full((HROWS, 1), -jnp.inf, f32), jnp.zeros((HROWS, 1), f32), jnp.zeros((HROWS, LANES), f32))

    ca, cm = lax.fori_loop(0, n_groups, attend_group, (init(), init()))

    def finish(carry, s_new, v_new):
        m, l, acc = carry
        m_new = jnp.maximum(m, s_new)
        m_safe = jnp.where(m_new == -jnp.inf, 0.0, m_new)
        p = jnp.exp(s_new - m_safe)
        corr = jnp.exp(m - m_safe)
        return (corr * acc + p * v_new) / (corr * l + p)

    kvn = kvn_ref[0].astype(f32)
    sa_new = jnp.sum(aq.astype(f32) * kvn[:, :LANES], axis=1, keepdims=True) + bias_ref[:, past:past + 1]
    sa_new = jnp.where(selected(key_new, jnp.int32(past)), sa_new, -jnp.inf)
    out_a = finish(ca, sa_new, kvn[:, LANES:])
    kmn = kmn_ref[0].astype(f32)
    sm_new = jnp.sum(qm.astype(f32) * kmn, axis=1, keepdims=True) * MLA_SCALE
    out_m = finish(cm, sm_new, kmn[:, :KV_RANK])
    lane = lax.broadcasted_iota(i32, (1, LANES), 1)
    blk_a = jnp.where(lane < HEAD_DIM, out_a[0:1], pltpu.roll(out_a[1:2], HEAD_DIM, 1))
    blk_b = jnp.where(lane < HEAD_DIM, pltpu.roll(out_a[2:3], HEAD_DIM, 1), out_a[3:4])
    oa_ref[0] = jnp.concatenate([blk_a, blk_b], axis=1).astype(oa_ref.dtype)
    o_lat = out_m.astype(bf16)
    ob = jnp.zeros((1, B_WIDTH), f32)
    for hd in range(B_HEADS):
        ob = ob + jnp.dot(o_lat, wuv_ref[hd], preferred_element_type=f32)[hd:hd + 1]
    ob_ref[0] = ob.astype(ob_ref.dtype)


def sample_attn(layer, page_table, iq, iw, aq, qm, ikn, kvn, kmn, bias, wuv, c_idx, c_lat, c_rope, c_ak, c_av, topk):
    n_seq, n_pages = page_table.shape
    assert n_pages % GROUP_PAGES == 0 and n_pages // GROUP_PAGES >= ATTN_AHEAD
    n_groups = n_pages // GROUP_PAGES
    rows = -(-n_groups // HROWS) * HROWS
    past = n_pages * PAGE_SIZE
    f32 = jnp.float32

    def sspec(r, w):
        return pl.BlockSpec((1, r, w), lambda b, pt: (b, 0, 0))

    def cspec(shape):
        nd = len(shape)
        return pl.BlockSpec(shape, lambda b, pt: (0,) * nd, pipeline_mode=pl.Buffered(1))

    hbm = pl.BlockSpec(memory_space=pl.ANY)
    return pl.pallas_call(
        functools.partial(_sample_attn_kernel, layer, topk, n_pages),
        grid_spec=pltpu.PrefetchScalarGridSpec(
            num_scalar_prefetch=1,
            grid=(n_seq,),
            in_specs=[sspec(HROWS, LANES), sspec(HROWS, 1), sspec(HROWS, LANES), sspec(HROWS, 2 * LANES),
                      sspec(1, LANES), sspec(1, 2 * LANES), sspec(1, 2 * LANES), cspec(bias.shape), cspec(wuv.shape),
                      hbm, hbm, hbm, hbm, hbm],
            out_specs=[sspec(1, A_WIDTH), sspec(1, B_WIDTH)],
            scratch_shapes=[pltpu.VMEM((2, IDX_DIM, past), f32), pltpu.VMEM((ATTN_SLOTS, GK, KV_RANK), f32),
                            pltpu.VMEM((ATTN_SLOTS, ROPE_DIM, GK), f32), pltpu.VMEM((ATTN_SLOTS, LANES, GK), f32),
                            pltpu.VMEM((ATTN_SLOTS, LANES, GK), f32), pltpu.VMEM((rows, GK), jnp.int32),
                            pltpu.SemaphoreType.DMA((2,)), pltpu.SemaphoreType.DMA((ATTN_SLOTS,))]),
        out_shape=[jax.ShapeDtypeStruct((n_seq, 1, A_WIDTH), jnp.bfloat16), jax.ShapeDtypeStruct((n_seq, 1, B_WIDTH), jnp.bfloat16)],
        compiler_params=_params(),
        name="sample_attn",
    )(page_table, iq, iw, aq, qm, ikn, kvn, kmn, bias, wuv, c_idx, c_lat, c_rope, c_ak, c_av)


def _s5_step_kernel(u_ref, sre_ref, sim_ref, are_ref, aim_ref, bre_ref, bim_ref, cre_ref, cim_ref, d_ref, y_ref, ore_ref, oim_ref):
    hi = lax.Precision.HIGHEST
    u = u_ref[...]
    s_re, s_im, a_re, a_im = sre_ref[...], sim_ref[...], are_ref[...], aim_ref[...]
    n_re = a_re * s_re - a_im * s_im + jnp.dot(u, bre_ref[...], precision=hi, preferred_element_type=jnp.float32)
    n_im = a_re * s_im + a_im * s_re + jnp.dot(u, bim_ref[...], precision=hi, preferred_element_type=jnp.float32)
    ore_ref[...] = n_re
    oim_ref[...] = n_im
    y_ref[...] = (jnp.dot(n_re, cre_ref[...], precision=hi, preferred_element_type=jnp.float32)
                  - jnp.dot(n_im, cim_ref[...], precision=hi, preferred_element_type=jnp.float32) + d_ref[...] * u)


def s5_step(u, s_re, s_im, ops):
    n = u.shape[0]
    n_state = C_GROUPS * C_STATE
    f32 = jnp.float32
    args = (u, s_re, s_im) + tuple(ops)
    return pl.pallas_call(
        _s5_step_kernel,
        grid=(1,),
        in_specs=[_const_spec(a.shape) for a in args],
        out_specs=[pl.BlockSpec((n, C_WIDTH), lambda i: (0, 0)), pl.BlockSpec((n, n_state), lambda i: (0, 0)),
                   pl.BlockSpec((n, n_state), lambda i: (0, 0))],
        out_shape=[jax.ShapeDtypeStruct((n, C_WIDTH), f32), jax.ShapeDtypeStruct((n, n_state), f32), jax.ShapeDtypeStruct((n, n_state), f32)],
        compiler_params=_params(),
        name="s5_step",
    )(*args)


def _s5_step_weights(lam_re, lam_im, log_dt, b_re, b_im, c_re, c_im, d_skip):
    f32 = jnp.float32
    _, a_bar, b_bar, c_mat = _s5_discretise(lam_re, lam_im, log_dt, b_re, b_im, c_re, c_im)
    eye = jnp.eye(C_GROUPS, dtype=f32)

    def bdiag_in(m):
        return jnp.einsum('gpc,gh->gchp', m, eye).reshape(C_WIDTH, C_GROUPS * C_STATE)

    def bdiag_out(m):
        return jnp.einsum('gcp,gh->gphc', m, eye).reshape(C_GROUPS * C_STATE, C_WIDTH)

    return (a_bar.real.reshape(1, -1), a_bar.imag.reshape(1, -1), bdiag_in(b_bar.real), bdiag_in(b_bar.imag),
            bdiag_out(c_mat.real), bdiag_out(c_mat.imag), d_skip.astype(f32).reshape(1, C_WIDTH))


def _sample_trunk(x, mod, p, page_table, caches, state_re, state_im):
    n = x.shape[0]
    n_pages = page_table.shape[1]
    past = n_pages * PAGE_SIZE
    xs = x.reshape(n, D_MODEL)
    cos, sin = _rope_tables(jnp.full((n,), past, jnp.int32))
    dist = past - jnp.arange(past + LANES, dtype=jnp.int32)
    bias = p['rel_bias'][_t5_bucket(dist)].astype(jnp.float32).T
    bias = jnp.concatenate([bias, jnp.zeros((HROWS - A_HEADS, past + LANES), jnp.float32)], axis=0)
    topk = min(TOPK_MAX, (past + 1) // 4)
    outs = []

    def head_rows(a, w):
        a = a.reshape(n, -1, w)
        return jnp.concatenate([a, jnp.zeros((n, HROWS - a.shape[1], w), a.dtype)], axis=1)

    for l in range(DEPTH):
        m = mod[l].reshape(n, 3, 3, D_MODEL)

        def mo(j, k):
            return m[:, j, k].reshape(1, n, D_MODEL)

        xs = ffn_block(xs, mo(0, 0), mo(0, 1), mo(0, 2), *p['ffn_w'][l][0], p['ln_g'][l, 0], p['ln_b'][l, 0], 1,
                       ln_in=(p['ln_in_g'], p['ln_in_b']) if l == 0 else None)
        wp, wuq, wuk = p['mix_w'][l]
        aq, ak, av, kv, iq, misc, ik, ckv, kpe, kmla, qmla, cu = mixproj(
            xs, mo(1, 0), mo(1, 1), wp, p['g_qnorm'][l], p['g_kvnorm'][l], wuq, wuk, cos, sin, 1)
        iw = head_rows(misc[:, IDX_DIM:IDX_DIM + IDX_HEADS], 1)
        oa, ob = sample_attn(l, page_table, head_rows(iq, LANES), iw, head_rows(aq, LANES), head_rows(qmla, 2 * LANES),
                             ik.reshape(n, 1, LANES), kv.reshape(n, 1, 2 * LANES), kmla.reshape(n, 1, 2 * LANES), bias, p['wuv'][l],
                             *caches, topk)
        ys, s_re, s_im = s5_step(cu, state_re[l].reshape(n, -1), state_im[l].reshape(n, -1), _s5_step_weights(*p['ssm'][l]))
        xs = outproj(xs, mo(1, 2), oa.reshape(n, A_WIDTH), ob.reshape(n, B_WIDTH), ys, p['w_glu'][l], p['b_glu'][l], p['w_out'][l],
                     p['ln_g'][l, 1], p['ln_b'][l, 1], 1)
        xs = ffn_block(xs, mo(2, 0), mo(2, 1), mo(2, 2), *p['ffn_w'][l][1], p['ln_g'][l, 2], p['ln_b'][l, 2], 1)
        outs.append((ak.reshape(n, 1, A_KV_HEADS, HEAD_DIM), av.reshape(n, 1, A_KV_HEADS, HEAD_DIM), misc[:, None, :IDX_DIM], ckv[:, None, :],
                     kpe[:, None, :ROPE_DIM], s_re.reshape(n, C_GROUPS, C_STATE), s_im.reshape(n, C_GROUPS, C_STATE)))
    stacked = [jnp.stack(s, axis=0) for s in zip(*outs)]
    return xs.reshape(n, 1, D_MODEL), stacked


def kernel(x_prompt, x_sample, c_prompt, c_sample, page_table, cache_attn_k, cache_attn_v, cache_idx_k, cache_mla_latent, cache_mla_rope, state_ssm_re, state_ssm_im, ln_in_g, ln_in_b, rel_bias, w_ada, b_ada, ln_g, ln_b, w_ffn_in, w_ffn_out, w_in, w_out, g_qnorm, g_kvnorm, w_uq, w_uk, w_uv, ssm_lam_re, ssm_lam_im, ssm_log_dt, ssm_b_re, ssm_b_im, ssm_c_re, ssm_c_im, ssm_d, w_glu, b_glu):
    n_p, n_s = c_prompt.shape[0], c_sample.shape[0]
    n_c = -(-(n_p + n_s) // 8) * 8
    c_all = jnp.concatenate([c_prompt, c_sample, jnp.zeros((n_c - n_p - n_s, D_MODEL), jnp.float32)], axis=0)
    mod_all = adaln(c_all, w_ada, b_ada)
    params = dict(
        ffn_w=[[_ffn_weights(w_ffn_in[l, j], w_ffn_out[l, j]) for j in range(2)] for l in range(DEPTH)],
        mix_w=[_mix_weights(w_in[l], w_uq[l], w_uk[l]) for l in range(DEPTH)],
        wuv=[_wuv_blocks(w_uv[l]) for l in range(DEPTH)],
        ssm=[(ssm_lam_re[l], ssm_lam_im[l], ssm_log_dt[l], ssm_b_re[l], ssm_b_im[l], ssm_c_re[l], ssm_c_im[l], ssm_d[l]) for l in range(DEPTH)],
        ln_in_g=ln_in_g, ln_in_b=ln_in_b, ln_g=ln_g, ln_b=ln_b, rel_bias=rel_bias, g_qnorm=g_qnorm, g_kvnorm=g_kvnorm,
        w_glu=w_glu, b_glu=b_glu, w_out=w_out)
    y_prompt, st_p = _prompt_trunk(x_prompt, mod_all[:, :n_p], params)
    n_pool = cache_attn_k.shape[1]
    caches = (cache_idx_k.transpose(0, 1, 3, 2), cache_mla_latent, cache_mla_rope.transpose(0, 1, 3, 2),
              cache_attn_k.transpose(0, 1, 3, 4, 2).reshape(DEPTH, n_pool, LANES, PAGE_SIZE),
              cache_attn_v.transpose(0, 1, 3, 4, 2).reshape(DEPTH, n_pool, LANES, PAGE_SIZE))
    y_sample, st_s = _sample_trunk(x_sample, mod_all[:, n_p:n_p + n_s], params, page_table, caches, state_ssm_re, state_ssm_im)
    return (y_prompt, y_sample, st_p[0], st_s[0], st_p[1], st_s[1], st_p[2], st_s[2], st_p[3], st_s[3], st_p[4], st_s[4], st_p[5], st_s[5], st_p[6], st_s[6])
```

```python
import functools
import math

import jax
import jax.numpy as jnp
import numpy as np
from jax import lax
from jax.experimental import pallas as pl
from jax.experimental.pallas import tpu as pltpu

D_MODEL = 1024
BATCH = 2
SEQ = 8192
DEPTH = 2
DEC_BATCH = 128
DEC_SEQ = 1
PAST_LEN = 16384
PAGE_SIZE = 128
HEAD_DIM = 64
A_HEADS = 4
A_KV_HEADS = 2
IDX_HEADS = 4
IDX_DIM = 32
TOPK_MAX = 256
REL_BUCKETS = 32
REL_MAX_DIST = 128
B_HEADS = 4
Q_RANK = 256
KV_RANK = 128
NOPE_DIM = 64
ROPE_DIM = 32
V_DIM = 64
ROPE_THETA = 10000.0
A_WIDTH = A_HEADS * HEAD_DIM
B_WIDTH = B_HEADS * V_DIM
C_WIDTH = D_MODEL - A_WIDTH - B_WIDTH
C_GROUP = 16
C_GROUPS = C_WIDTH // C_GROUP
C_STATE = 64
D_FF = 2816
Q_BLOCK = 128
LN_EPS = 1e-5
ALPHA = (2.0 * DEPTH) ** 0.25
MLA_SCALE = (NOPE_DIM + ROPE_DIM) ** -0.5
IN_SPLITS = (A_HEADS * HEAD_DIM, A_KV_HEADS * HEAD_DIM, A_KV_HEADS * HEAD_DIM, IDX_HEADS * IDX_DIM, IDX_HEADS, IDX_DIM, Q_RANK, KV_RANK, ROPE_DIM, C_WIDTH)
D_IN = sum(IN_SPLITS)

VMEM_LIMIT_V7X = 56 * 1024 * 1024
LANES = 128
FF_CHUNK = 256
N_FF_CHUNKS = D_FF // FF_CHUNK
ROWS = 512
TQ = 128
TQ_MLA = 256
TK_MLA = 1024
TK = 512
TKA = TK
S5_CHUNK = 16
INT_MIN = -2 ** 31

COL_AQ = 0
COL_AK = COL_AQ + A_HEADS * LANES
COL_AV = COL_AK + LANES
COL_IQ = COL_AV + LANES
COL_CQ = COL_IQ + IDX_HEADS * LANES
COL_CKV = COL_CQ + Q_RANK
COL_CU = COL_CKV + KV_RANK
COL_KA = COL_CU + C_WIDTH
COL_KB = COL_KA + LANES
COL_MISC = COL_KB + LANES
N_PROJ = COL_MISC + LANES
UQ_NOPE = 0
UQ_A = B_HEADS * NOPE_DIM
UQ_B = UQ_A + B_HEADS * LANES
N_UQ = UQ_B + B_HEADS * LANES


def _params(n_axes=1):
    return pltpu.CompilerParams(dimension_semantics=("arbitrary",) * n_axes, vmem_limit_bytes=VMEM_LIMIT_V7X)


def _ln(v, g, b):
    mu = jnp.mean(v, -1, keepdims=True)
    d = v - mu
    var = jnp.mean(d * d, -1, keepdims=True)
    return d * lax.rsqrt(var + LN_EPS) * g + b


def _mod_spec(mod, rows, rows_per_group):
    if mod.shape[1] == 1:
        return pl.BlockSpec((1, 1, D_MODEL), lambda i: (i // (rows_per_group // rows), 0, 0))
    return pl.BlockSpec((1, rows, D_MODEL), lambda i: (0, i, 0))


def _const_spec(shape):
    nd = len(shape)
    return pl.BlockSpec(shape, lambda *_: (0,) * nd, pipeline_mode=pl.Buffered(1))


def _to_key(score):
    bits = lax.bitcast_convert_type(score + 0.0, jnp.int32)
    return jnp.where(bits < 0, bits ^ 0x7FFFFFFF, bits)


ADA_COLS = 1152


def _adaln_kernel(c_ref, w_ref, b_ref, o_ref):
    c = c_ref[...]
    o_ref[0] = jnp.dot(c * jax.nn.sigmoid(c), w_ref[0], preferred_element_type=jnp.float32) + b_ref[0]


def adaln(c, w_ada, b_ada):
    n = c.shape[0]
    n_out = w_ada.shape[2]
    return pl.pallas_call(
        _adaln_kernel,
        grid=(DEPTH, n_out // ADA_COLS),
        in_specs=[pl.BlockSpec((n, D_MODEL), lambda l, j: (0, 0)),
                  pl.BlockSpec((1, D_MODEL, ADA_COLS), lambda l, j: (l, 0, j)),
                  pl.BlockSpec((1, 1, ADA_COLS), lambda l, j: (l, 0, j))],
        out_specs=pl.BlockSpec((1, n, ADA_COLS), lambda l, j: (l, 0, j)),
        out_shape=jax.ShapeDtypeStruct((DEPTH, n, n_out), jnp.float32),
        compiler_params=_params(2),
        name="adaln",
    )(c, w_ada, b_ada.reshape(DEPTH, 1, n_out))


def _ffn_kernel(pre_ln, x_ref, shift_ref, scale_ref, gate_ref, wg_ref, wu_ref, wd_ref, lng_ref, lnb_ref, *rest):
    if pre_ln:
        ing_ref, inb_ref, o_ref, acc_ref = rest
    else:
        o_ref, acc_ref = rest
    x = x_ref[...]
    if pre_ln:
        x = _ln(x, ing_ref[...], inb_ref[...])
    h = (x * (1.0 + scale_ref[0]) + shift_ref[0]).astype(jnp.bfloat16)
    acc_ref[...] = jnp.zeros_like(acc_ref)

    def chunk(c, carry):
        g = jnp.dot(h, wg_ref[c], preferred_element_type=jnp.float32)
        u = jnp.dot(h, wu_ref[c], preferred_element_type=jnp.float32)
        act = (g * jax.nn.sigmoid(g) * u).astype(jnp.bfloat16)
        acc_ref[...] += jnp.dot(act, wd_ref[c], preferred_element_type=jnp.float32)
        return carry

    lax.fori_loop(0, N_FF_CHUNKS, chunk, 0)
    y = ALPHA * x + 0.5 * (1.0 + gate_ref[0]) * acc_ref[...]
    o_ref[...] = _ln(y, lng_ref[...], lnb_ref[...])


def ffn_block(x, shift, scale, gate, wg, wu, wd, ln_g, ln_b, rows_per_group, ln_in=None):
    n = x.shape[0]
    rows = min(ROWS, n)
    pre_ln = ln_in is not None
    row_spec = pl.BlockSpec((rows, D_MODEL), lambda i: (i, 0))
    vec_spec = _const_spec((1, D_MODEL))
    in_specs = [row_spec] + [_mod_spec(m, rows, rows_per_group) for m in (shift, scale, gate)] + [
        _const_spec(wg.shape), _const_spec(wu.shape), _const_spec(wd.shape), vec_spec, vec_spec]
    args = [x, shift, scale, gate, wg, wu, wd, ln_g.reshape(1, D_MODEL), ln_b.reshape(1, D_MODEL)]
    if pre_ln:
        in_specs += [vec_spec, vec_spec]
        args += [ln_in[0].reshape(1, D_MODEL), ln_in[1].reshape(1, D_MODEL)]
    return pl.pallas_call(
        functools.partial(_ffn_kernel, pre_ln),
        grid=(n // rows,),
        in_specs=in_specs,
        out_specs=row_spec,
        out_shape=jax.ShapeDtypeStruct((n, D_MODEL), jnp.float32),
        scratch_shapes=[pltpu.VMEM((rows, D_MODEL), jnp.float32)],
        compiler_params=_params(),
        name="ffn_block",
    )(*args)


def _ffn_weights(w_in, w_out):
    wg = w_in[:, :D_FF].reshape(D_MODEL, N_FF_CHUNKS, FF_CHUNK).transpose(1, 0, 2).astype(jnp.bfloat16)
    wu = w_in[:, D_FF:].reshape(D_MODEL, N_FF_CHUNKS, FF_CHUNK).transpose(1, 0, 2).astype(jnp.bfloat16)
    wd = w_out.reshape(N_FF_CHUNKS, FF_CHUNK, D_MODEL).astype(jnp.bfloat16)
    return wg, wu, wd


def _mixproj_kernel(x_ref, shift_ref, scale_ref, wp_ref, gq_ref, gkv_ref, wuq_ref, wuk_ref, cos_ref, sin_ref,
                    aq_ref, ak_ref, av_ref, kv_ref, iq_ref, misc_ref, ik_ref, ckv_ref, kpe_ref, kmla_ref, qmla_ref, cu_ref):
    bf16 = jnp.bfloat16
    h = (x_ref[...] * (1.0 + scale_ref[0]) + shift_ref[0]).astype(bf16)
    p = jnp.dot(h, wp_ref[...], preferred_element_type=jnp.float32)
    aq_ref[...] = (p[:, COL_AQ:COL_AK] * (HEAD_DIM ** -0.5)).astype(bf16)
    ak_ref[...] = p[:, COL_AK:COL_AV]
    av_ref[...] = p[:, COL_AV:COL_IQ]
    kv_ref[...] = p[:, COL_AK:COL_IQ].astype(bf16)
    iq_ref[...] = (p[:, COL_IQ:COL_CQ] * (IDX_DIM ** -0.5)).astype(bf16)
    lane = lax.broadcasted_iota(jnp.int32, (1, LANES), 1)
    is_w = (lane >= IDX_DIM) & (lane < IDX_DIM + IDX_HEADS)
    misc = p[:, COL_MISC:N_PROJ] * jnp.where(is_w, IDX_HEADS ** -0.5, 1.0)
    misc_ref[...] = misc
    ik_ref[...] = misc.astype(bf16)
    cos = cos_ref[...]
    sin = sin_ref[...]
    kpe = p[:, COL_KA:COL_KB] * cos + p[:, COL_KB:COL_MISC] * sin
    kpe_ref[...] = kpe
    ckv_in = p[:, COL_CKV:COL_CU]
    ckv = ckv_in * lax.rsqrt(jnp.mean(ckv_in * ckv_in, -1, keepdims=True) + 1e-6) * gkv_ref[...]
    ckv_ref[...] = ckv
    kmla_ref[...] = jnp.concatenate([ckv, kpe], axis=1).astype(bf16)
    cq_in = p[:, COL_CQ:COL_CKV]
    cq = (cq_in * lax.rsqrt(jnp.mean(cq_in * cq_in, -1, keepdims=True) + 1e-6) * gq_ref[...]).astype(bf16)
    q = jnp.dot(cq, wuq_ref[...], preferred_element_type=jnp.float32)
    q_lat = jnp.dot(q[:, UQ_NOPE:UQ_A].astype(bf16), wuk_ref[...], preferred_element_type=jnp.float32)
    pieces = []
    for hd in range(B_HEADS):
        q_pe = q[:, UQ_A + hd * LANES:UQ_A + (hd + 1) * LANES] * cos + q[:, UQ_B + hd * LANES:UQ_B + (hd + 1) * LANES] * sin
        pieces += [q_lat[:, hd * KV_RANK:(hd + 1) * KV_RANK], q_pe]
    qmla_ref[...] = jnp.concatenate(pieces, axis=1).astype(bf16)
    cu_ref[...] = p[:, COL_CU:COL_KA]


def mixproj(x, shift, scale, wp, gq, gkv, wuq, wuk, cos, sin, rows_per_group):
    n = x.shape[0]
    rows = min(ROWS, n)
    f32, bf16 = jnp.float32, jnp.bfloat16

    def rspec(w):
        return pl.BlockSpec((rows, w), lambda i: (i, 0))

    out_w = [(A_HEADS * LANES, bf16), (LANES, f32), (LANES, f32), (2 * LANES, bf16), (IDX_HEADS * LANES, bf16),
             (LANES, f32), (LANES, bf16), (KV_RANK, f32), (LANES, f32), (2 * LANES, bf16),
             (B_HEADS * 2 * LANES, bf16), (C_WIDTH, f32)]
    return pl.pallas_call(
        _mixproj_kernel,
        grid=(n // rows,),
        in_specs=[rspec(D_MODEL), _mod_spec(shift, rows, rows_per_group), _mod_spec(scale, rows, rows_per_group),
                  _const_spec(wp.shape), _const_spec((1, Q_RANK)), _const_spec((1, KV_RANK)),
                  _const_spec(wuq.shape), _const_spec(wuk.shape), rspec(LANES), rspec(LANES)],
        out_specs=[rspec(w) for w, _ in out_w],
        out_shape=[jax.ShapeDtypeStruct((n, w), dt) for w, dt in out_w],
        compiler_params=_params(),
        name="mixproj",
    )(x, shift, scale, wp, gq.reshape(1, Q_RANK), gkv.reshape(1, KV_RANK), wuq, wuk, cos, sin)


def _mix_weights(w_in, w_uq, w_uk):
    f32 = jnp.float32
    off = np.concatenate([[0], np.cumsum(IN_SPLITS)])
    a_q, a_k, a_v, i_q, i_w, i_k, b_cq, b_ckv, b_kpe, c_u = [w_in[:, off[i]:off[i + 1]] for i in range(10)]
    wp = jnp.zeros((D_MODEL, N_PROJ), f32)
    grp = A_HEADS // A_KV_HEADS
    for hd in range(A_HEADS):
        lo = COL_AQ + hd * LANES + HEAD_DIM * (hd // grp)
        wp = wp.at[:, lo:lo + HEAD_DIM].set(a_q[:, hd * HEAD_DIM:(hd + 1) * HEAD_DIM])
    wp = wp.at[:, COL_AK:COL_AK + LANES].set(a_k)
    wp = wp.at[:, COL_AV:COL_AV + LANES].set(a_v)
    for hd in range(IDX_HEADS):
        lo = COL_IQ + hd * LANES
        wp = wp.at[:, lo:lo + IDX_DIM].set(i_q[:, hd * IDX_DIM:(hd + 1) * IDX_DIM])
    wp = wp.at[:, COL_CQ:COL_CQ + Q_RANK].set(b_cq)
    wp = wp.at[:, COL_CKV:COL_CKV + KV_RANK].set(b_ckv)
    wp = wp.at[:, COL_CU:COL_CU + C_WIDTH].set(c_u)
    half = ROPE_DIM // 2
    wp = wp.at[:, COL_KA:COL_KA + ROPE_DIM].set(b_kpe)
    wp = wp.at[:, COL_KB:COL_KB + half].set(-b_kpe[:, half:])
    wp = wp.at[:, COL_KB + half:COL_KB + ROPE_DIM].set(b_kpe[:, :half])
    wp = wp.at[:, COL_MISC:COL_MISC + IDX_DIM].set(i_k)
    wp = wp.at[:, COL_MISC + IDX_DIM:COL_MISC + IDX_DIM + IDX_HEADS].set(i_w)
    wuq = jnp.zeros((Q_RANK, N_UQ), f32)
    wuk = jnp.zeros((B_HEADS * NOPE_DIM, B_HEADS * KV_RANK), f32)
    for hd in range(B_HEADS):
        wuq = wuq.at[:, UQ_NOPE + hd * NOPE_DIM:UQ_NOPE + (hd + 1) * NOPE_DIM].set(w_uq[:, hd, :NOPE_DIM])
        r = w_uq[:, hd, NOPE_DIM:]
        wuq = wuq.at[:, UQ_A + hd * LANES:UQ_A + hd * LANES + ROPE_DIM].set(r)
        wuq = wuq.at[:, UQ_B + hd * LANES:UQ_B + hd * LANES + half].set(-r[:, half:])
        wuq = wuq.at[:, UQ_B + hd * LANES + half:UQ_B + hd * LANES + ROPE_DIM].set(r[:, :half])
        wuk = wuk.at[hd * NOPE_DIM:(hd + 1) * NOPE_DIM, hd * KV_RANK:(hd + 1) * KV_RANK].set(w_uk[:, hd, :].T)
    return wp.astype(jnp.bfloat16), wuq.astype(jnp.bfloat16), wuk.astype(jnp.bfloat16)


def _rope_tables(pos):
    half = ROPE_DIM // 2
    freqs = ROPE_THETA ** (-jnp.arange(half, dtype=jnp.float32) / half)
    ang = pos.astype(jnp.float32)[:, None] * freqs
    z = jnp.zeros((pos.shape[0], LANES - ROPE_DIM), jnp.float32)
    return (jnp.concatenate([jnp.cos(ang), jnp.cos(ang), z], 1), jnp.concatenate([jnp.sin(ang), jnp.sin(ang), z], 1))


def _idx_attn_kernel(topk, iq_ref, misc_ref, ik_ref, aq_ref, kv_ref, bnear_ref, bfar_ref, o_ref, keys_ref):
    f32, bf16, i32 = jnp.float32, jnp.bfloat16, jnp.int32
    qi = pl.program_id(1)
    t0 = qi * TQ
    n_c = (t0 + TQ + TK - 1) // TK
    n_keys = keys_ref.shape[1]
    t_pos = t0 + lax.broadcasted_iota(i32, (TQ, 1), 0)
    misc = misc_ref[0]
    nt = (((1,), (1,)), ((), ()))

    @pl.when((pl.program_id(0) == 0) & (qi == 0))
    def _():
        keys_ref[...] = jnp.full(keys_ref.shape, INT_MIN, i32)

    iq4 =jnp.concatenate([iq_ref[0, :, hd * LANES:(hd + 1) * LANES] for hd in range(IDX_HEADS)], axis=0)
    w4 = jnp.stack([misc[:, IDX_DIM + hd:IDX_DIM + hd + 1] for hd in range(IDX_HEADS)], axis=0)

    def score_chunk(c, carry):
        base = pl.multiple_of(c * TK, TK)
        kb = ik_ref[0, pl.ds(base, TK), :]
        s = lax.dot_general(iq4, kb, nt, preferred_element_type=f32).reshape(IDX_HEADS, TQ, TK)
        sc = jnp.sum(jnp.maximum(s, 0.0) * w4, axis=0)
        s_pos = base + lax.broadcasted_iota(i32, (TQ, TK), 1)
        keys_ref[:, pl.ds(base, TK)] = _to_key(jnp.where(s_pos <= t_pos, sc, -jnp.inf))
        return carry

    lax.fori_loop(0, n_c, score_chunk, 0)

    def fold(m):
        return m[:, 0:LANES] + m[:, LANES:2 * LANES] + m[:, 2 * LANES:3 * LANES] + m[:, 3 * LANES:4 * LANES]

    def count(pred):
        def body(c, acc):
            base = pl.multiple_of(c * TK, TK)
            k = keys_ref[:, pl.ds(base, TK)]
            idx = base + lax.broadcasted_iota(i32, (TQ, TK), 1)
            return acc + fold(jnp.where(pred(k, idx), 1.0, 0.0))
        acc = lax.fori_loop(0, n_c, body, jnp.zeros((TQ, LANES), f32))
        return jnp.sum(acc, axis=1, keepdims=True)

    kf = float(topk)

    ans0 = jnp.where(count(lambda k, idx: k >= 0) >= kf, 0, INT_MIN).astype(i32)

    def bit_step(i, ans):
        cand = ans | jnp.left_shift(jnp.int32(1), 30 - i)
        return jnp.where(count(lambda k, idx: k >= cand) >= kf, cand, ans)

    tau = lax.fori_loop(0, 31, bit_step, ans0)
    n_gt = count(lambda k, idx: k > tau)
    n_eq = count(lambda k, idx: k == tau)
    need = kf - n_gt

    idx_bits = n_keys.bit_length()

    def tie_search(_):
        def step(i, x):
            cand = x | jnp.left_shift(jnp.int32(1), idx_bits - 1 - i)
            return jnp.where(count(lambda k, idx: (k == tau) & (idx < cand)) < need, cand, x)
        return lax.fori_loop(0, idx_bits, step, jnp.zeros((TQ, 1), i32))

    any_excess = jnp.max(jnp.where(n_eq > need, 1.0, 0.0)) > 0.5
    j_cut = lax.cond(any_excess, tie_search, lambda _: jnp.full((TQ, 1), n_keys, i32), 0)

    q4 = jnp.concatenate([aq_ref[0, :, hd * LANES:(hd + 1) * LANES] for hd in range(A_HEADS)], axis=0)
    near0 = t0 - TQ

    def attend(carry, kv, k, idx, bias, limit_lo, limit_hi):
        m, l, acc = carry
        sel = ((k > tau) | ((k == tau) & (idx <= j_cut))) & (idx >= limit_lo) & (idx < limit_hi) & (idx <= t_pos)
        w = kv.shape[0]
        s = lax.dot_general(q4, kv[:, :LANES], nt, preferred_element_type=f32).reshape(A_HEADS, TQ, w)
        s = jnp.where(sel[None], s + bias, -jnp.inf)
        m_new = jnp.maximum(m, jnp.max(s, axis=-1, keepdims=True))
        m_safe = jnp.where(m_new == -jnp.inf, 0.0, m_new)
        p = jnp.exp(s - m_safe)
        corr = jnp.exp(m - m_safe)
        l = corr * l + jnp.sum(p, axis=-1, keepdims=True)
        pv = jnp.dot(p.reshape(A_HEADS * TQ, w).astype(bf16), kv[:, LANES:], preferred_element_type=f32)
        acc = corr * acc + pv.reshape(A_HEADS, TQ, LANES)
        return m_new, l, acc

    carry0 = (jnp.full((A_HEADS, TQ, 1), -jnp.inf, f32), jnp.zeros((A_HEADS, TQ, 1), f32), jnp.zeros((A_HEADS, TQ, LANES), f32))
    bias_far = bfar_ref[...]

    def far_chunk(c, carry):
        base = pl.multiple_of(c * TKA, TKA)
        idx = base + lax.broadcasted_iota(i32, (TQ, TKA), 1)
        return attend(carry, kv_ref[0, pl.ds(base, TKA), :], keys_ref[:, pl.ds(base, TKA)], idx, bias_far[:, :, :1], 0, near0)

    carry = lax.fori_loop(0, (jnp.maximum(near0, 0) + TKA - 1) // TKA, far_chunk, carry0)
    prev = pl.multiple_of(jnp.maximum(near0, 0), TQ)
    diag = pl.multiple_of(t0, TQ)
    kv_near = jnp.concatenate([kv_ref[0, pl.ds(prev, TQ), :], kv_ref[0, pl.ds(diag, TQ), :]], axis=0)
    k_near = jnp.concatenate([keys_ref[:, pl.ds(prev, TQ)], keys_ref[:, pl.ds(diag, TQ)]], axis=1)
    idx_near = near0 + lax.broadcasted_iota(i32, (TQ, 2 * TQ), 1)
    m, l, acc = attend(carry, kv_near, k_near, idx_near, bnear_ref[...], 0, n_keys)
    out = acc / l
    lane = lax.broadcasted_iota(i32, (TQ, LANES), 1)
    blk_a = jnp.where(lane < HEAD_DIM, out[0], pltpu.roll(out[1], HEAD_DIM, 1))
    blk_b = jnp.where(lane < HEAD_DIM, pltpu.roll(out[2], HEAD_DIM, 1), out[3])
    o_ref[0] = jnp.concatenate([blk_a, blk_b], axis=1).astype(o_ref.dtype)


def idx_attn(iq, misc, ik, aq, kv, bias_near, bias_far, topk):
    b_, t_ = iq.shape[0], iq.shape[1]
    assert t_ % TKA == 0 and t_ % TK == 0

    def qspec(w):
        return pl.BlockSpec((1, TQ, w), lambda b, i: (b, i, 0))

    def kspec(w):
        return pl.BlockSpec((1, t_, w), lambda b, i: (b, 0, 0))

    return pl.pallas_call(
        functools.partial(_idx_attn_kernel, topk),
        grid=(b_, t_ // TQ),
        in_specs=[qspec(IDX_HEADS * LANES), qspec(LANES), kspec(LANES), qspec(A_HEADS * LANES), kspec(2 * LANES),
                  _const_spec(bias_near.shape), _const_spec(bias_far.shape)],
        out_specs=qspec(A_WIDTH),
        out_shape=jax.ShapeDtypeStruct((b_, t_, A_WIDTH), jnp.bfloat16),
        scratch_shapes=[pltpu.VMEM((TQ, t_), jnp.int32)],
        compiler_params=_params(2),
        name="idx_attn",
    )(iq, misc, ik, aq, kv, bias_near, bias_far)


def _t5_bucket(dist):
    n = jnp.maximum(dist, 0)
    max_exact = REL_BUCKETS // 2
    n_f = jnp.maximum(n, 1).astype(jnp.float32)
    log_bucket = max_exact + (jnp.log(n_f / max_exact) / math.log(REL_MAX_DIST / max_exact) * (REL_BUCKETS - max_exact)).astype(jnp.int32)
    return jnp.where(n < max_exact, n, jnp.minimum(log_bucket, REL_BUCKETS - 1))


def _bias_tables(rel_bias):
    assert TQ >= REL_MAX_DIST
    dist = TQ + jnp.arange(TQ, dtype=jnp.int32)[:, None] - jnp.arange(2 * TQ, dtype=jnp.int32)[None, :]
    near = rel_bias[_t5_bucket(dist)].astype(jnp.float32).transpose(2, 0, 1)
    far = rel_bias[_t5_bucket(jnp.int32(2 * TQ))].astype(jnp.float32)
    return near, jnp.broadcast_to(far[:, None, None], (A_HEADS, 1, LANES))


def _mla_kernel(q_ref, k_ref, wuv_ref, o_ref):
    f32, bf16, i32 = jnp.float32, jnp.bfloat16, jnp.int32
    TQ, TK = TQ_MLA, TK_MLA
    t0 = pl.program_id(1) * TQ
    w = 2 * LANES
    q4 = jnp.concatenate([q_ref[0, :, hd * w:(hd + 1) * w] for hd in range(B_HEADS)], axis=0)
    t_pos = t0 + lax.broadcasted_iota(i32, (TQ, 1), 0)

    def chunk(c, carry):
        m, l, acc = carry
        base = pl.multiple_of(c * TK, TK)
        kb = k_ref[0, pl.ds(base, TK), :]
        s = lax.dot_general(q4, kb, (((1,), (1,)), ((), ())), preferred_element_type=f32).reshape(B_HEADS, TQ, TK) * MLA_SCALE
        s_pos = base + lax.broadcasted_iota(i32, (TQ, TK), 1)
        s = jnp.where((s_pos <= t_pos)[None], s, -jnp.inf)
        m_new = jnp.maximum(m, jnp.max(s, axis=-1, keepdims=True))
        p = jnp.exp(s - m_new)
        corr = jnp.exp(m - m_new)
        l = corr * l + jnp.sum(p, axis=-1, keepdims=True)
        pv = jnp.dot(p.reshape(B_HEADS * TQ, TK).astype(bf16), kb[:, :KV_RANK], preferred_element_type=f32)
        return m_new, l, corr * acc + pv.reshape(B_HEADS, TQ, KV_RANK)

    carry0 = (jnp.full((B_HEADS, TQ, 1), -jnp.inf, f32), jnp.zeros((B_HEADS, TQ, 1), f32), jnp.zeros((B_HEADS, TQ, KV_RANK), f32))
    m, l, acc = lax.fori_loop(0, (t0 + TQ + TK - 1) // TK, chunk, carry0)
    o_lat = (acc / l).astype(bf16)
    out = jnp.zeros((TQ, B_WIDTH), f32)
    for hd in range(B_HEADS):
        out = out + jnp.dot(o_lat[hd], wuv_ref[hd], preferred_element_type=f32)
    o_ref[0] = out.astype(o_ref.dtype)


def mla_attn(qmla, kmla, wuv):
    b_, t_ = qmla.shape[0], qmla.shape[1]
    TQ = TQ_MLA
    return pl.pallas_call(
        _mla_kernel,
        grid=(b_, t_ // TQ),
        in_specs=[pl.BlockSpec((1, TQ, qmla.shape[2]), lambda b, i: (b, i, 0)),
                  pl.BlockSpec((1, t_, 2 * LANES), lambda b, i: (b, 0, 0)),
                  _const_spec(wuv.shape)],
        out_specs=pl.BlockSpec((1, TQ, B_WIDTH), lambda b, i: (b, i, 0)),
        out_shape=jax.ShapeDtypeStruct((b_, t_, B_WIDTH), jnp.bfloat16),
        compiler_params=_params(2),
        name="mla_attn",
    )(qmla, kmla, wuv)


def _wuv_blocks(w_uv):
    out = jnp.zeros((B_HEADS, KV_RANK, B_WIDTH), jnp.float32)
    for hd in range(B_HEADS):
        out = out.at[hd, :, hd * V_DIM:(hd + 1) * V_DIM].set(w_uv[:, hd, :])
    return out.astype(jnp.bfloat16)


def _s5_kernel(n_levels, chunks_per_seq, u_ref, w_ref, wy_ref, apow_ref, y_ref, s_ref):
    f32, i32 = jnp.float32, jnp.int32
    n_rows = u_ref.shape[1]
    cw = S5_CHUNK * C_GROUP
    ye = jnp.dot(u_ref[0], w_ref[0], preferred_element_type=f32)
    e = ye[:, cw:]
    row = lax.broadcasted_iota(i32, (n_rows, 1), 0) % chunks_per_seq
    lane = lax.broadcasted_iota(i32, (1, 2 * C_STATE), 1)
    sign = jnp.where(lane < C_STATE, -1.0, 1.0)
    s = e
    for lv in range(n_levels):
        d = 1 << lv
        a = apow_ref[0, lv:lv + 1, :]
        a_re = jnp.concatenate([a[:, :C_STATE], a[:, :C_STATE]], axis=1)
        a_im = jnp.concatenate([a[:, C_STATE:], a[:, C_STATE:]], axis=1) * sign
        sh = jnp.where(row >= d, pltpu.roll(s, d, 0), 0.0)
        s = s + sh * a_re + pltpu.roll(sh, C_STATE, 1) * a_im
    s_ref[0] = s
    s_prev = jnp.where(row >= 1, pltpu.roll(s, 1, 0), 0.0)
    y_ref[0] = ye[:, :cw] + jnp.dot(s_prev.astype(jnp.bfloat16), wy_ref[0], preferred_element_type=f32)


def s5_prompt(u_g, w_cat, w_y, a_pow, chunks_per_seq):
    g_, n_rows, cw = u_g.shape
    n_levels = a_pow.shape[1]
    return pl.pallas_call(
        functools.partial(_s5_kernel, n_levels, chunks_per_seq),
        grid=(g_,),
        in_specs=[pl.BlockSpec((1, n_rows, cw), lambda g: (g, 0, 0)),
                  pl.BlockSpec((1, cw, cw + 2 * C_STATE), lambda g: (g, 0, 0)),
                  pl.BlockSpec((1, 2 * C_STATE, cw), lambda g: (g, 0, 0)),
                  pl.BlockSpec((1, n_levels, 2 * C_STATE), lambda g: (g, 0, 0))],
        out_specs=[pl.BlockSpec((1, n_rows, cw), lambda g: (g, 0, 0)),
                   pl.BlockSpec((1, n_rows, 2 * C_STATE), lambda g: (g, 0, 0))],
        out_shape=[jax.ShapeDtypeStruct((g_, n_rows, cw), jnp.float32),
                   jax.ShapeDtypeStruct((g_, n_rows, 2 * C_STATE), jnp.float32)],
        compiler_params=_params(),
        name="s5_prompt",
    )(u_g, w_cat, w_y, a_pow)


def _s5_discretise(lam_re, lam_im, log_dt, b_re, b_im, c_re, c_im):
    f32 = jnp.float32
    lam = lax.complex(lam_re.astype(f32), lam_im.astype(f32))
    dt = jnp.exp(log_dt.astype(f32))[:, None]
    a_bar = jnp.exp(lam * dt)
    b_bar = ((a_bar - 1.0) / lam)[..., None] * lax.complex(b_re.astype(f32), b_im.astype(f32))
    c_mat = lax.complex(c_re.astype(f32), c_im.astype(f32))
    return lam * dt, a_bar, b_bar, c_mat


def _s5_weights(lam_re, lam_im, log_dt, b_re, b_im, c_re, c_im, d_skip, n_levels):
    f32 = jnp.float32
    L = S5_CHUNK
    lam_dt, _, b_bar, c_mat = _s5_discretise(lam_re, lam_im, log_dt, b_re, b_im, c_re, c_im)
    taus = jnp.arange(L + 1, dtype=f32)
    a_pw = jnp.exp(lam_dt[None] * taus[:, None, None])
    kern = jnp.einsum('gcp,tgp,gpd->tgcd', c_mat, a_pw[:L], b_bar).real
    t_idx = jnp.arange(L)
    diff = t_idx[None, :] - t_idx[:, None]
    toe = kern[jnp.clip(diff, 0, L - 1)]
    toe = jnp.where((diff >= 0)[:, :, None, None, None], toe, 0.0)
    eye = (diff == 0)[:, :, None, None, None] * (jnp.eye(C_GROUP, dtype=f32)[None, None, None] * d_skip.astype(f32)[None, None, :, :, None])
    toe = (toe + eye).transpose(2, 0, 4, 1, 3).reshape(C_GROUPS, L * C_GROUP, L * C_GROUP)
    e_c = a_pw[L - 1 - t_idx][:, :, :, None] * b_bar[None]
    e_c = e_c.transpose(1, 0, 3, 2).reshape(C_GROUPS, L * C_GROUP, C_STATE)
    w_cat = jnp.concatenate([toe, e_c.real, e_c.imag], axis=2)
    y_c = c_mat[None] * a_pw[1:L + 1][:, :, None, :]
    y_c = y_c.transpose(1, 3, 0, 2).reshape(C_GROUPS, C_STATE, L * C_GROUP)
    w_y = jnp.concatenate([y_c.real, -y_c.imag], axis=1)
    lv = (L * (2.0 ** jnp.arange(n_levels, dtype=f32)))
    a_lv = jnp.exp(lam_dt[:, None, :] * lv[None, :, None])
    a_pow = jnp.concatenate([a_lv.real, a_lv.imag], axis=2)
    return w_cat.astype(jnp.bfloat16), w_y.astype(jnp.bfloat16), a_pow


def _outproj_kernel(x_ref, gate_ref, oa_ref, ob_ref, ys_ref, wglu_ref, bglu_ref, woa_ref, wob_ref, woc_ref, lng_ref, lnb_ref, o_ref):
    f32, bf16 = jnp.float32, jnp.bfloat16
    y = ys_ref[...]
    z = 0.5 * y * (1.0 + jnp.tanh(math.sqrt(2.0 / math.pi) * (y + 0.044715 * (y * y * y))))
    yc = z * jax.nn.sigmoid(jnp.dot(z.astype(bf16), wglu_ref[...], preferred_element_type=f32) + bglu_ref[...])
    mix = jnp.dot(oa_ref[...], woa_ref[...], preferred_element_type=f32)
    mix = mix + jnp.dot(ob_ref[...], wob_ref[...], preferred_element_type=f32)
    mix = mix + jnp.dot(yc.astype(bf16), woc_ref[...], preferred_element_type=f32)
    o_ref[...] = _ln(ALPHA * x_ref[...] + (1.0 + gate_ref[0]) * mix, lng_ref[...], lnb_ref[...])


def outproj(x, gate, oa, ob, ys, wglu, bglu, w_out, ln_g, ln_b, rows_per_group):
    n = x.shape[0]
    rows = min(ROWS, n)
    bf16 = jnp.bfloat16

    def rspec(w):
        return pl.BlockSpec((rows, w), lambda i: (i, 0))

    woa, wob, woc = w_out[:A_WIDTH].astype(bf16), w_out[A_WIDTH:A_WIDTH + B_WIDTH].astype(bf16), w_out[A_WIDTH + B_WIDTH:].astype(bf16)
    vec = _const_spec((1, D_MODEL))
    return pl.pallas_call(
        _outproj_kernel,
        grid=(n // rows,),
        in_specs=[rspec(D_MODEL), _mod_spec(gate, rows, rows_per_group), rspec(A_WIDTH), rspec(B_WIDTH), rspec(C_WIDTH),
                  _const_spec((C_WIDTH, C_WIDTH)), _const_spec((1, C_WIDTH)),
                  _const_spec(woa.shape), _const_spec(wob.shape), _const_spec(woc.shape), vec, vec],
        out_specs=rspec(D_MODEL),
        out_shape=jax.ShapeDtypeStruct((n, D_MODEL), jnp.float32),
        compiler_params=_params(),
        name="outproj",
    )(x, gate, oa, ob, ys, wglu.astype(bf16), bglu.reshape(1, C_WIDTH), woa, wob, woc, ln_g.reshape(1, D_MODEL), ln_b.reshape(1, D_MODEL))


def _prompt_trunk(x, mod, p):
    b_, t_ = x.shape[0], x.shape[1]
    n = b_ * t_
    xf = x.reshape(n, D_MODEL)
    pos = jnp.arange(t_, dtype=jnp.int32)
    cos, sin = _rope_tables(pos)
    cos, sin = jnp.tile(cos, (b_, 1)), jnp.tile(sin, (b_, 1))
    bias_near, bias_far = _bias_tables(p['rel_bias'])
    topk = min(TOPK_MAX, t_ // 4)
    n_chunks = t_ // S5_CHUNK
    n_levels = max(1, (n_chunks - 1).bit_length())
    outs = []
    for l in range(DEPTH):
        m = mod[l].reshape(b_, 3, 3, 1, D_MODEL)

        def mo(j, k):
            return m[:, j, k]

        xf = ffn_block(xf, mo(0, 0), mo(0, 1), mo(0, 2), *p['ffn_w'][l][0], p['ln_g'][l, 0], p['ln_b'][l, 0], t_,
                       ln_in=(p['ln_in_g'], p['ln_in_b']) if l == 0 else None)
        wp, wuq, wuk = p['mix_w'][l]
        aq, ak, av, kv, iq, misc, ik, ckv, kpe, kmla, qmla, cu = mixproj(
            xf, mo(1, 0), mo(1, 1), wp, p['g_qnorm'][l], p['g_kvnorm'][l], wuq, wuk, cos, sin, t_)

        def seq(a):
            return a.reshape(b_, t_, a.shape[-1])

        oa = idx_attn(seq(iq), seq(misc), seq(ik), seq(aq), seq(kv), bias_near, bias_far, topk)
        ob = mla_attn(seq(qmla), seq(kmla), p['wuv'][l])
        w_cat, w_y, a_pow = _s5_weights(*p['ssm'][l], n_levels)
        u_g = cu.astype(jnp.bfloat16).reshape(b_ * n_chunks, S5_CHUNK, C_GROUPS, C_GROUP).transpose(2, 0, 1, 3).reshape(
            C_GROUPS, b_ * n_chunks, S5_CHUNK * C_GROUP)
        y_g, s_g = s5_prompt(u_g, w_cat, w_y, a_pow, n_chunks)
        ys = y_g.reshape(C_GROUPS, b_ * n_chunks, S5_CHUNK, C_GROUP).transpose(1, 2, 0, 3).reshape(n, C_WIDTH)
        s_last = s_g.reshape(C_GROUPS, b_, n_chunks, 2 * C_STATE)[:, :, -1].transpose(1, 0, 2)
        xf = outproj(xf, mo(1, 2), oa.reshape(n, A_WIDTH), ob.reshape(n, B_WIDTH), ys, p['w_glu'][l], p['b_glu'][l], p['w_out'][l],
                     p['ln_g'][l, 1], p['ln_b'][l, 1], t_)
        xf = ffn_block(xf, mo(2, 0), mo(2, 1), mo(2, 2), *p['ffn_w'][l][1], p['ln_g'][l, 2], p['ln_b'][l, 2], t_)
        outs.append((ak.reshape(b_, t_, A_KV_HEADS, HEAD_DIM), av.reshape(b_, t_, A_KV_HEADS, HEAD_DIM),
                     seq(misc)[:, :, :IDX_DIM], seq(ckv), seq(kpe)[:, :, :ROPE_DIM], s_last[:, :, :C_STATE], s_last[:, :, C_STATE:]))
    stacked = [jnp.stack(s, axis=0) for s in zip(*outs)]
    return xf.reshape(b_, t_, D_MODEL), stacked


GROUP_PAGES = 16
GK = GROUP_PAGES * PAGE_SIZE
HROWS = 8
ATTN_AHEAD = 2
ATTN_SLOTS = ATTN_AHEAD + 1


def _sample_attn_kernel(layer, topk, n_pages, pt_ref, iq_ref, iw_ref, aq_ref, qm_ref, ikn_ref, kvn_ref, kmn_ref, bias_ref, wuv_ref,
                        idx_hbm, lat_hbm, rope_hbm, ak_hbm, av_hbm, oa_ref, ob_ref,
                        idx_buf, lat_buf, rope_buf, ak_buf, av_buf, keys_ref, sem_i, sem_a):
    f32, bf16, i32 = jnp.float32, jnp.bfloat16, jnp.int32
    b = pl.program_id(0)
    n_seq = pl.num_programs(0)
    n_groups = n_pages // GROUP_PAGES
    past = n_pages * PAGE_SIZE
    nt = (((1,), (1,)), ((), ()))

    def idx_copies(seq, slot):
        return [pltpu.make_async_copy(idx_hbm.at[layer, pt_ref[seq, j]], idx_buf.at[slot, :, pl.ds(j * PAGE_SIZE, PAGE_SIZE)], sem_i.at[slot])
                for j in range(n_pages)]

    def attn_copies(seq, g, slot):
        out = []
        for j in range(GROUP_PAGES):
            phys = pt_ref[seq, g * GROUP_PAGES + j]
            cols = pl.ds(j * PAGE_SIZE, PAGE_SIZE)
            out.append(pltpu.make_async_copy(ak_hbm.at[layer, phys], ak_buf.at[slot, :, cols], sem_a.at[slot]))
            out.append(pltpu.make_async_copy(av_hbm.at[layer, phys], av_buf.at[slot, :, cols], sem_a.at[slot]))
            out.append(pltpu.make_async_copy(rope_hbm.at[layer, phys], rope_buf.at[slot, :, cols], sem_a.at[slot]))
            out.append(pltpu.make_async_copy(lat_hbm.at[layer, phys], lat_buf.at[slot, cols, :], sem_a.at[slot]))
        return out

    islot = b % 2

    @pl.when(b == 0)
    def _():
        for c in idx_copies(0, 0):
            c.start()
        for g in range(ATTN_AHEAD):
            for c in attn_copies(0, g, g):
                c.start()

    @pl.when(b + 1 < n_seq)
    def _():
        for c in idx_copies(b + 1, 1 - islot):
            c.start()

    for c in idx_copies(b, islot):
        c.wait()

    iq = iq_ref[0]
    iw = iw_ref[0]
    keys_ref[...] = jnp.full(keys_ref.shape, INT_MIN + 1, i32)
    for g in range(n_groups):
        kb = idx_buf[islot, :, g * GK:(g + 1) * GK].astype(bf16)
        s = jnp.dot(iq[:, :IDX_DIM], kb, preferred_element_type=f32)
        keys_ref[g:g + 1, :] = _to_key(jnp.sum(jnp.maximum(s, 0.0) * iw, axis=0, keepdims=True))
    s_new =jnp.sum(iq.astype(f32) * ikn_ref[0].astype(f32), axis=1, keepdims=True)
    key_new = _to_key(jnp.sum(jnp.maximum(s_new, 0.0) * iw, axis=0, keepdims=True))

    rows = keys_ref.shape[0]
    idx_all = lax.broadcasted_iota(i32, (rows, GK), 0) * GK + lax.broadcasted_iota(i32, (rows, GK), 1)

    def count(pred):
        k = keys_ref[...]
        c = jnp.sum(jnp.where(pred(k, idx_all), 1.0, 0.0), axis=1, keepdims=True)
        return jnp.sum(c, axis=0, keepdims=True) + jnp.where(pred(key_new, jnp.int32(past)), 1.0, 0.0)

    kf = float(topk)
    ans0 = jnp.where(count(lambda k, idx: k >= 0) >= kf, 0, INT_MIN).astype(i32)

    def bit_pair(i, ans):
        hi = jnp.left_shift(jnp.int32(1), 29 - 2 * i)
        lo = jnp.left_shift(jnp.int32(1), 28 - 2 * i)
        c_hi, c_both, c_lo = ans | hi, ans | hi | lo, ans | lo
        n_hi = count(lambda k, idx: k >= c_hi)
        n_both = count(lambda k, idx: k >= c_both)
        n_lo = count(lambda k, idx: k >= c_lo)
        return jnp.where(n_both >= kf, c_both, jnp.where(n_hi >= kf, c_hi, jnp.where(n_lo >= kf, c_lo, ans)))

    cand30 = ans0 | jnp.left_shift(jnp.int32(1), 30)
    ans1 = jnp.where(count(lambda k, idx: k >= cand30) >= kf, cand30, ans0)
    tau = lax.fori_loop(0, 15, bit_pair, ans1)
    need = kf - count(lambda k, idx: k > tau)
    n_eq = count(lambda k, idx: k == tau)
    idx_bits = (past + 1).bit_length()

    def tie_search(_):
        def step(i, x):
            cand = x | jnp.left_shift(jnp.int32(1), idx_bits - 1 - i)
            return jnp.where(count(lambda k, idx: (k == tau) & (idx < cand)) < need, cand, x)
        return lax.fori_loop(0, idx_bits, step, jnp.zeros((1, 1), i32))

    j_cut = lax.cond(jnp.max(jnp.where(n_eq > need, 1.0, 0.0)) > 0.5, tie_search, lambda _: jnp.full((1, 1), past + 1, i32), 0)

    def selected(k, idx):
        return (k > tau) | ((k == tau) & (idx <= j_cut))

    aq = aq_ref[0]
    qm = qm_ref[0]

    def online(carry, s, pv_fn):
        m, l, acc = carry
        m_new = jnp.maximum(m, jnp.max(s, axis=-1, keepdims=True))
        m_safe = jnp.where(m_new == -jnp.inf, 0.0, m_new)
        p = jnp.exp(s - m_safe)
        corr = jnp.exp(m - m_safe)
        return m_new, corr * l + jnp.sum(p, axis=-1, keepdims=True), corr * acc + pv_fn(p.astype(bf16))

    def attend_group(g, carry):
        ca, cm = carry
        gg = b * n_groups + g
        slot = gg % ATTN_SLOTS
        ahead = gg + ATTN_AHEAD
        seq_a = ahead // n_groups

        @pl.when(seq_a < n_seq)
        def _():
            for c in attn_copies(seq_a, ahead % n_groups, ahead % ATTN_SLOTS):
                c.start()

        for c in attn_copies(b, g, slot):
            c.wait()
        base = pl.multiple_of(g * GK, GK)
        k = keys_ref[pl.ds(g, 1), :]
        idx = base + lax.broadcasted_iota(i32, (1, GK), 1)
        sa = jnp.dot(aq, ak_buf[slot].astype(bf16), preferred_element_type=f32) + bias_ref[:, pl.ds(base, GK)]
        v_t = av_buf[slot].astype(bf16)
        ca = online(ca, jnp.where(selected(k, idx), sa, -jnp.inf), lambda p: lax.dot_general(p, v_t, nt, preferred_element_type=f32))
        lat = lat_buf[slot].astype(bf16)
        sm = lax.dot_general(qm[:, :KV_RANK], lat, nt, preferred_element_type=f32)
        sm = sm + jnp.dot(qm[:, KV_RANK:KV_RANK + ROPE_DIM], rope_buf[slot].astype(bf16), preferred_element_type=f32)
        cm = online(cm, sm * MLA_SCALE, lambda p: jnp.dot(p, lat, preferred_element_type=f32))
        return ca, cm

    def init():
        return (jnp.full((HROWS, 1), -jnp.inf, f32), jnp.zeros((HROWS, 1), f32), jnp.zeros((HROWS, LANES), f32))

    ca, cm = lax.fori_loop(0, n_groups, attend_group, (init(), init()))

    def finish(carry, s_new, v_new):
        m, l, acc = carry
        m_new = jnp.maximum(m, s_new)
        m_safe = jnp.where(m_new == -jnp.inf, 0.0, m_new)
        p = jnp.exp(s_new - m_safe)
        corr = jnp.exp(m - m_safe)
        return (corr * acc + p * v_new) / (corr * l + p)

    kvn = kvn_ref[0].astype(f32)
    sa_new = jnp.sum(aq.astype(f32) * kvn[:, :LANES], axis=1, keepdims=True) + bias_ref[:, past:past + 1]
    sa_new = jnp.where(selected(key_new, jnp.int32(past)), sa_new, -jnp.inf)
    out_a = finish(ca, sa_new, kvn[:, LANES:])
    kmn = kmn_ref[0].astype(f32)
    sm_new = jnp.sum(qm.astype(f32) * kmn, axis=1, keepdims=True) * MLA_SCALE
    out_m = finish(cm, sm_new, kmn[:, :KV_RANK])
    lane = lax.broadcasted_iota(i32, (1, LANES), 1)
    blk_a = jnp.where(lane < HEAD_DIM, out_a[0:1], pltpu.roll(out_a[1:2], HEAD_DIM, 1))
    blk_b = jnp.where(lane < HEAD_DIM, pltpu.roll(out_a[2:3], HEAD_DIM, 1), out_a[3:4])
    oa_ref[0] = jnp.concatenate([blk_a, blk_b], axis=1).astype(oa_ref.dtype)
    o_lat = out_m.astype(bf16)
    ob = jnp.zeros((1, B_WIDTH), f32)
    for hd in range(B_HEADS):
        ob = ob + jnp.dot(o_lat, wuv_ref[hd], preferred_element_type=f32)[hd:hd + 1]
    ob_ref[0] = ob.astype(ob_ref.dtype)


def sample_attn(layer, page_table, iq, iw, aq, qm, ikn, kvn, kmn, bias, wuv, c_idx, c_lat, c_rope, c_ak, c_av, topk):
    n_seq, n_pages = page_table.shape
    assert n_pages % GROUP_PAGES == 0 and n_pages // GROUP_PAGES >= ATTN_AHEAD
    n_groups = n_pages // GROUP_PAGES
    rows = -(-n_groups // HROWS) * HROWS
    past = n_pages * PAGE_SIZE
    f32 = jnp.float32

    def sspec(r, w):
        return pl.BlockSpec((1, r, w), lambda b, pt: (b, 0, 0))

    def cspec(shape):
        nd = len(shape)
        return pl.BlockSpec(shape, lambda b, pt: (0,) * nd, pipeline_mode=pl.Buffered(1))

    hbm = pl.BlockSpec(memory_space=pl.ANY)
    return pl.pallas_call(
        functools.partial(_sample_attn_kernel, layer, topk, n_pages),
        grid_spec=pltpu.PrefetchScalarGridSpec(
            num_scalar_prefetch=1,
            grid=(n_seq,),
            in_specs=[sspec(HROWS, LANES), sspec(HROWS, 1), sspec(HROWS, LANES), sspec(HROWS, 2 * LANES),
                      sspec(1, LANES), sspec(1, 2 * LANES), sspec(1, 2 * LANES), cspec(bias.shape), cspec(wuv.shape),
                      hbm, hbm, hbm, hbm, hbm],
            out_specs=[sspec(1, A_WIDTH), sspec(1, B_WIDTH)],
            scratch_shapes=[pltpu.VMEM((2, IDX_DIM, past), f32), pltpu.VMEM((ATTN_SLOTS, GK, KV_RANK), f32),
                            pltpu.VMEM((ATTN_SLOTS, ROPE_DIM, GK), f32), pltpu.VMEM((ATTN_SLOTS, LANES, GK), f32),
                            pltpu.VMEM((ATTN_SLOTS, LANES, GK), f32), pltpu.VMEM((rows, GK), jnp.int32),
                            pltpu.SemaphoreType.DMA((2,)), pltpu.SemaphoreType.DMA((ATTN_SLOTS,))]),
        out_shape=[jax.ShapeDtypeStruct((n_seq, 1, A_WIDTH), jnp.bfloat16), jax.ShapeDtypeStruct((n_seq, 1, B_WIDTH), jnp.bfloat16)],
        compiler_params=_params(),
        name="sample_attn",
    )(page_table, iq, iw, aq, qm, ikn, kvn, kmn, bias, wuv, c_idx, c_lat, c_rope, c_ak, c_av)


def _s5_step_kernel(u_ref, sre_ref, sim_ref, are_ref, aim_ref, bre_ref, bim_ref, cre_ref, cim_ref, d_ref, y_ref, ore_ref, oim_ref):
    hi = lax.Precision.HIGHEST
    u = u_ref[...]
    s_re, s_im, a_re, a_im = sre_ref[...], sim_ref[...], are_ref[...], aim_ref[...]
    n_re = a_re * s_re - a_im * s_im + jnp.dot(u, bre_ref[...], precision=hi, preferred_element_type=jnp.float32)
    n_im = a_re * s_im + a_im * s_re + jnp.dot(u, bim_ref[...], precision=hi, preferred_element_type=jnp.float32)
    ore_ref[...] = n_re
    oim_ref[...] = n_im
    y_ref[...] = (jnp.dot(n_re, cre_ref[...], precision=hi, preferred_element_type=jnp.float32)
                  - jnp.dot(n_im, cim_ref[...], precision=hi, preferred_element_type=jnp.float32) + d_ref[...] * u)


def s5_step(u, s_re, s_im, ops):
    n = u.shape[0]
    n_state = C_GROUPS * C_STATE
    f32 = jnp.float32
    args = (u, s_re, s_im) + tuple(ops)
    return pl.pallas_call(
        _s5_step_kernel,
        grid=(1,),
        in_specs=[_const_spec(a.shape) for a in args],
        out_specs=[pl.BlockSpec((n, C_WIDTH), lambda i: (0, 0)), pl.BlockSpec((n, n_state), lambda i: (0, 0)),
                   pl.BlockSpec((n, n_state), lambda i: (0, 0))],
        out_shape=[jax.ShapeDtypeStruct((n, C_WIDTH), f32), jax.ShapeDtypeStruct((n, n_state), f32), jax.ShapeDtypeStruct((n, n_state), f32)],
        compiler_params=_params(),
        name="s5_step",
    )(*args)


def _s5_step_weights(lam_re, lam_im, log_dt, b_re, b_im, c_re, c_im, d_skip):
    f32 = jnp.float32
    _, a_bar, b_bar, c_mat = _s5_discretise(lam_re, lam_im, log_dt, b_re, b_im, c_re, c_im)
    eye = jnp.eye(C_GROUPS, dtype=f32)

    def bdiag_in(m):
        return jnp.einsum('gpc,gh->gchp', m, eye).reshape(C_WIDTH, C_GROUPS * C_STATE)

    def bdiag_out(m):
        return jnp.einsum('gcp,gh->gphc', m, eye).reshape(C_GROUPS * C_STATE, C_WIDTH)

    return (a_bar.real.reshape(1, -1), a_bar.imag.reshape(1, -1), bdiag_in(b_bar.real), bdiag_in(b_bar.imag),
            bdiag_out(c_mat.real), bdiag_out(c_mat.imag), d_skip.astype(f32).reshape(1, C_WIDTH))


def _sample_trunk(x, mod, p, page_table, caches, state_re, state_im):
    n = x.shape[0]
    n_pages = page_table.shape[1]
    past = n_pages * PAGE_SIZE
    xs = x.reshape(n, D_MODEL)
    cos, sin = _rope_tables(jnp.full((n,), past, jnp.int32))
    assert 2 * LANES >= REL_MAX_DIST and past >= 2 * LANES
    near = p['rel_bias'][_t5_bucket(jnp.arange(2 * LANES - 1, -1, -1, dtype=jnp.int32))].astype(jnp.float32).T
    far = p['rel_bias'][_t5_bucket(jnp.int32(2 * LANES))].astype(jnp.float32)
    bias = jnp.concatenate([jnp.broadcast_to(far[:, None], (A_HEADS, past + 1 - 2 * LANES)), near,
                            jnp.zeros((A_HEADS, LANES - 1), jnp.float32)], axis=1)
    bias = jnp.concatenate([bias, jnp.zeros((HROWS - A_HEADS, past + LANES), jnp.float32)], axis=0)
    topk = min(TOPK_MAX, (past + 1) // 4)
    outs = []

    def head_rows(a, w):
        a = a.reshape(n, -1, w)
        return jnp.concatenate([a, jnp.zeros((n, HROWS - a.shape[1], w), a.dtype)], axis=1)

    for l in range(DEPTH):
        m = mod[l].reshape(n, 3, 3, D_MODEL)

        def mo(j, k):
            return m[:, j, k].reshape(1, n, D_MODEL)

        xs = ffn_block(xs, mo(0, 0), mo(0, 1), mo(0, 2), *p['ffn_w'][l][0], p['ln_g'][l, 0], p['ln_b'][l, 0], 1,
                       ln_in=(p['ln_in_g'], p['ln_in_b']) if l == 0 else None)
        wp, wuq, wuk = p['mix_w'][l]
        aq, ak, av, kv, iq, misc, ik, ckv, kpe, kmla, qmla, cu = mixproj(
            xs, mo(1, 0), mo(1, 1), wp, p['g_qnorm'][l], p['g_kvnorm'][l], wuq, wuk, cos, sin, 1)
        iw = head_rows(misc[:, IDX_DIM:IDX_DIM + IDX_HEADS], 1)
        oa, ob = sample_attn(l, page_table, head_rows(iq, LANES), iw, head_rows(aq, LANES), head_rows(qmla, 2 * LANES),
                             ik.reshape(n, 1, LANES), kv.reshape(n, 1, 2 * LANES), kmla.reshape(n, 1, 2 * LANES), bias, p['wuv'][l],
                             *caches, topk)
        ys, s_re, s_im = s5_step(cu, state_re[l].reshape(n, -1), state_im[l].reshape(n, -1), _s5_step_weights(*p['ssm'][l]))
        xs = outproj(xs, mo(1, 2), oa.reshape(n, A_WIDTH), ob.reshape(n, B_WIDTH), ys, p['w_glu'][l], p['b_glu'][l], p['w_out'][l],
                     p['ln_g'][l, 1], p['ln_b'][l, 1], 1)
        xs = ffn_block(xs, mo(2, 0), mo(2, 1), mo(2, 2), *p['ffn_w'][l][1], p['ln_g'][l, 2], p['ln_b'][l, 2], 1)
        outs.append((ak.reshape(n, 1, A_KV_HEADS, HEAD_DIM), av.reshape(n, 1, A_KV_HEADS, HEAD_DIM), misc[:, None, :IDX_DIM], ckv[:, None, :],
                     kpe[:, None, :ROPE_DIM], s_re.reshape(n, C_GROUPS, C_STATE), s_im.reshape(n, C_GROUPS, C_STATE)))
    stacked = [jnp.stack(s, axis=0) for s in zip(*outs)]
    return xs.reshape(n, 1, D_MODEL), stacked


def kernel(x_prompt, x_sample, c_prompt, c_sample, page_table, cache_attn_k, cache_attn_v, cache_idx_k, cache_mla_latent, cache_mla_rope, state_ssm_re, state_ssm_im, ln_in_g, ln_in_b, rel_bias, w_ada, b_ada, ln_g, ln_b, w_ffn_in, w_ffn_out, w_in, w_out, g_qnorm, g_kvnorm, w_uq, w_uk, w_uv, ssm_lam_re, ssm_lam_im, ssm_log_dt, ssm_b_re, ssm_b_im, ssm_c_re, ssm_c_im, ssm_d, w_glu, b_glu):
    n_p, n_s = c_prompt.shape[0], c_sample.shape[0]
    n_c = -(-(n_p + n_s) // 8) * 8
    c_all = jnp.concatenate([c_prompt, c_sample, jnp.zeros((n_c - n_p - n_s, D_MODEL), jnp.float32)], axis=0)
    mod_all = adaln(c_all, w_ada, b_ada)
    params = dict(
        ffn_w=[[_ffn_weights(w_ffn_in[l, j], w_ffn_out[l, j]) for j in range(2)] for l in range(DEPTH)],
        mix_w=[_mix_weights(w_in[l], w_uq[l], w_uk[l]) for l in range(DEPTH)],
        wuv=[_wuv_blocks(w_uv[l]) for l in range(DEPTH)],
        ssm=[(ssm_lam_re[l], ssm_lam_im[l], ssm_log_dt[l], ssm_b_re[l], ssm_b_im[l], ssm_c_re[l], ssm_c_im[l], ssm_d[l]) for l in range(DEPTH)],
        ln_in_g=ln_in_g, ln_in_b=ln_in_b, ln_g=ln_g, ln_b=ln_b, rel_bias=rel_bias, g_qnorm=g_qnorm, g_kvnorm=g_kvnorm,
        w_glu=w_glu, b_glu=b_glu, w_out=w_out)
    y_prompt, st_p = _prompt_trunk(x_prompt, mod_all[:, :n_p], params)
    n_pool = cache_attn_k.shape[1]
    caches = (cache_idx_k.transpose(0, 1, 3, 2), cache_mla_latent, cache_mla_rope.transpose(0, 1, 3, 2),
              cache_attn_k.transpose(0, 1, 3, 4, 2).reshape(DEPTH, n_pool, LANES, PAGE_SIZE),
              cache_attn_v.transpose(0, 1, 3, 4, 2).reshape(DEPTH, n_pool, LANES, PAGE_SIZE))
    y_sample, st_s = _sample_trunk(x_sample, mod_all[:, n_p:n_p + n_s], params, page_table, caches, state_ssm_re, state_ssm_im)
    return (y_prompt, y_sample, st_p[0], st_s[0], st_p[1], st_s[1], st_p[2], st_s[2], st_p[3], st_s[3], st_p[4], st_s[4], st_p[5], st_s[5], st_p[6], st_s[6])
```
